```python
import math
import jax, jax.numpy as jnp
from jax import lax
import numpy as np

D_MODEL = 1024
BATCH = 8
SEQ = 2048
DEPTH = 4

N_MIXERS = 3
CHUNK = 64
Q_BLOCK = 128
RMS_EPS = 1e-6
LN_EPS = 1e-5
CONV_WIDTH = D_MODEL
CONV_K = 31
FOX_HEAD_DIM = 64
FOX_HEADS = D_MODEL // FOX_HEAD_DIM
FOX_WIDTH = FOX_HEADS * FOX_HEAD_DIM
SSM_WIDTH = D_MODEL
SSM_GROUP = 16
SSM_GROUPS = SSM_WIDTH // SSM_GROUP
SSM_STATE = 64
DT_MIN = 1e-3
DT_MAX = 1e-1

N_CONV_LAYERS = len(range(0, DEPTH, N_MIXERS))
N_FOX_LAYERS = len(range(1, DEPTH, N_MIXERS))
N_SSM_LAYERS = len(range(2, DEPTH, N_MIXERS))

kernel_name = "interleaved_conv_fox_s5_trunk"


def rms_norm(x, g):
    xf = x.astype(jnp.float32)
    y = xf * lax.rsqrt(jnp.mean(xf * xf, axis=-1, keepdims=True) + RMS_EPS)
    return (y * g.astype(jnp.float32)).astype(x.dtype)


def head_rms(x, g):
    xf = x.astype(jnp.float32)
    return xf * lax.rsqrt(jnp.mean(xf * xf, axis=-1, keepdims=True) + RMS_EPS) * g.astype(jnp.float32)


def conv_mixer(h, w_in, conv_w, conv_b, ln_g, ln_b, w_out):
    z = h @ w_in
    a, b, gate = jnp.split(z, 3, axis=-1)
    u = a * jax.nn.sigmoid(b)
    u = lax.conv_general_dilated(
        u, conv_w[:, None, :], window_strides=(1,), padding=[(CONV_K - 1, 0)],
        dimension_numbers=("NWC", "WIO", "NWC"), feature_group_count=CONV_WIDTH,
    ) + conv_b
    uf = u.astype(jnp.float32)
    mu = jnp.mean(uf, axis=-1, keepdims=True)
    var = jnp.mean(jnp.square(uf - mu), axis=-1, keepdims=True)
    uf = (uf - mu) * lax.rsqrt(var + LN_EPS) * ln_g.astype(jnp.float32) + ln_b.astype(jnp.float32)
    u = jax.nn.silu(uf).astype(h.dtype) * jax.nn.silu(gate)
    return u @ w_out


def fox_mixer(h, w_in, f_bias, q_g, k_g, w_out):
    bsz, seq, _ = h.shape
    z = h @ w_in
    q, k, v, gate, f_logit = jnp.split(
        z, [FOX_WIDTH, 2 * FOX_WIDTH, 3 * FOX_WIDTH, 4 * FOX_WIDTH], axis=-1)
    shp = (bsz, seq, FOX_HEADS, FOX_HEAD_DIM)
    q = head_rms(q.reshape(shp), q_g)
    k = head_rms(k.reshape(shp), k_g)
    v = v.reshape(shp).astype(jnp.float32)
    log_f = jax.nn.log_sigmoid(f_logit.astype(jnp.float32) + f_bias.astype(jnp.float32))
    cum = jnp.cumsum(log_f, axis=1).transpose(0, 2, 1)
    scale = 1.0 / math.sqrt(FOX_HEAD_DIM)
    outs = []
    for blk in range(seq // Q_BLOCK):
        q0 = blk * Q_BLOCK
        q1 = q0 + Q_BLOCK
        s = jnp.einsum("bqhd,bkhd->bhqk", q[:, q0:q1], k[:, :q1]) * scale
        decay = cum[:, :, q0:q1, None] - cum[:, :, None, :q1]
        causal = jnp.arange(q0, q1)[:, None] >= jnp.arange(q1)[None, :]
        s = jnp.where(causal, s + decay, -jnp.inf)
        p = jax.nn.softmax(s, axis=-1)
        outs.append(jnp.einsum("bhqk,bkhd->bqhd", p, v[:, :q1]))
    o = jnp.concatenate(outs, axis=1).reshape(bsz, seq, FOX_WIDTH).astype(h.dtype)
    o = o * jax.nn.silu(gate)
    return o @ w_out


def _complex_affine_combine(e1, e2):
    a1r, a1i, b1r, b1i = e1
    a2r, a2i, b2r, b2i = e2
    ar = a2r * a1r - a2i * a1i
    ai = a2r * a1i + a2i * a1r
    br = a2r * b1r - a2i * b1i + b2r
    bi = a2r * b1i + a2i * b1r + b2i
    return (ar, ai, br, bi)


def ssm_mixer(h, w_in, log_dt, a_re, a_im, b_re, b_im, c_re, c_im, d_skip, w_glu, b_glu, w_out):
    bsz, seq, _ = h.shape
    f32 = jnp.float32
    z = h @ w_in
    u, gate = jnp.split(z, 2, axis=-1)
    uf = u.astype(f32)
    a_re = a_re.astype(f32)
    a_im = a_im.astype(f32)
    dt = jnp.exp(log_dt.astype(f32))[:, None]
    mag = jnp.exp(a_re * dt)
    ang = a_im * dt
    abar_re = mag * jnp.cos(ang)
    abar_im = mag * jnp.sin(ang)
    den = a_re * a_re + a_im * a_im
    nr = abar_re - 1.0
    ni = abar_im
    zr = ((nr * a_re + ni * a_im) / den)[..., None]
    zi = ((ni * a_re - nr * a_im) / den)[..., None]
    b_re = b_re.astype(f32)
    b_im = b_im.astype(f32)
    bb_re = zr * b_re - zi * b_im
    bb_im = zr * b_im + zi * b_re
    ug = uf.reshape(bsz, seq, SSM_GROUPS, SSM_GROUP)
    bu_re = jnp.einsum("bsgc,gpc->sbgp", ug, bb_re)
    bu_im = jnp.einsum("bsgc,gpc->sbgp", ug, bb_im)
    ar_t = jnp.broadcast_to(abar_re, (seq, 1, SSM_GROUPS, SSM_STATE))
    ai_t = jnp.broadcast_to(abar_im, (seq, 1, SSM_GROUPS, SSM_STATE))
    _, _, xr, xi = lax.associative_scan(
        _complex_affine_combine, (ar_t, ai_t, bu_re, bu_im), axis=0)
    y = (jnp.einsum("sbgp,gcp->bsgc", xr, c_re.astype(f32))
         - jnp.einsum("sbgp,gcp->bsgc", xi, c_im.astype(f32)))
    y = y.reshape(bsz, seq, SSM_WIDTH) + d_skip.astype(f32) * uf
    g = jax.nn.gelu(y)
    y = g * jax.nn.sigmoid(g @ w_glu.astype(f32) + b_glu.astype(f32))
    y = y.astype(h.dtype) * jax.nn.silu(gate)
    return y @ w_out


def setup_inputs(seed: int = 0) -> dict:
    key = jax.random.key(seed)
    ks = jax.random.split(key, 32)
    nrm = jax.random.normal
    D = D_MODEL
    x = nrm(ks[0], (BATCH, SEQ, D), jnp.float32)
    norm_g = 1.0 + 0.02 * nrm(ks[1], (DEPTH, D), jnp.float32)
    a_w_in = nrm(ks[2], (N_CONV_LAYERS, D, 3 * CONV_WIDTH), jnp.float32) * D ** -0.5
    a_conv_w = nrm(ks[3], (N_CONV_LAYERS, CONV_K, CONV_WIDTH), jnp.float32) * CONV_K ** -0.5
    a_conv_b = 0.02 * nrm(ks[4], (N_CONV_LAYERS, CONV_WIDTH), jnp.float32)
    a_ln_g = 1.0 + 0.02 * nrm(ks[5], (N_CONV_LAYERS, CONV_WIDTH), jnp.float32)
    a_ln_b = 0.02 * nrm(ks[6], (N_CONV_LAYERS, CONV_WIDTH), jnp.float32)
    a_w_out = nrm(ks[7], (N_CONV_LAYERS, CONV_WIDTH, D), jnp.float32) * CONV_WIDTH ** -0.5
    b_w_in = nrm(ks[8], (N_FOX_LAYERS, D, 4 * FOX_WIDTH + FOX_HEADS), jnp.float32) * D ** -0.5
    b_f_bias = jax.random.uniform(ks[9], (N_FOX_LAYERS, FOX_HEADS), jnp.float32, 1.0, 5.0)
    b_q_norm = 1.0 + 0.02 * nrm(ks[10], (N_FOX_LAYERS, FOX_HEAD_DIM), jnp.float32)
    b_k_norm = 1.0 + 0.02 * nrm(ks[11], (N_FOX_LAYERS, FOX_HEAD_DIM), jnp.float32)
    b_w_out = nrm(ks[12], (N_FOX_LAYERS, FOX_WIDTH, D), jnp.float32) * FOX_WIDTH ** -0.5
    G, P, Cg = SSM_GROUPS, SSM_STATE, SSM_GROUP
    c_w_in = nrm(ks[13], (N_SSM_LAYERS, D, 2 * SSM_WIDTH), jnp.float32) * D ** -0.5
    c_log_dt = jax.random.uniform(ks[14], (N_SSM_LAYERS, G), jnp.float32,
                                  math.log(DT_MIN), math.log(DT_MAX))
    c_a_re = -0.5 + 0.01 * nrm(ks[15], (N_SSM_LAYERS, G, P), jnp.float32)
    c_a_im = (jnp.pi * jnp.arange(P, dtype=jnp.float32))[None, None, :] \
        + 0.01 * nrm(ks[16], (N_SSM_LAYERS, G, P), jnp.float32)
    c_b_re = nrm(ks[17], (N_SSM_LAYERS, G, P, Cg), jnp.float32) * (2 * Cg) ** -0.5
    c_b_im = nrm(ks[18], (N_SSM_LAYERS, G, P, Cg), jnp.float32) * (2 * Cg) ** -0.5
    c_c_re = nrm(ks[19], (N_SSM_LAYERS, G, Cg, P), jnp.float32) * 0.5
    c_c_im = nrm(ks[20], (N_SSM_LAYERS, G, Cg, P), jnp.float32) * 0.5
    c_d = nrm(ks[21], (N_SSM_LAYERS, SSM_WIDTH), jnp.float32)
    c_w_glu = nrm(ks[22], (N_SSM_LAYERS, SSM_WIDTH, SSM_WIDTH), jnp.float32) * SSM_WIDTH ** -0.5
    c_b_glu = 0.02 * nrm(ks[23], (N_SSM_LAYERS, SSM_WIDTH), jnp.float32)
    c_w_out = nrm(ks[24], (N_SSM_LAYERS, SSM_WIDTH, D), jnp.float32) * SSM_WIDTH ** -0.5
    return {
        "x": x, "norm_g": norm_g,
        "a_w_in": a_w_in, "a_conv_w": a_conv_w, "a_conv_b": a_conv_b,
        "a_ln_g": a_ln_g, "a_ln_b": a_ln_b, "a_w_out": a_w_out,
        "b_w_in": b_w_in, "b_f_bias": b_f_bias, "b_q_norm": b_q_norm,
        "b_k_norm": b_k_norm, "b_w_out": b_w_out,
        "c_w_in": c_w_in, "c_log_dt": c_log_dt, "c_a_re": c_a_re, "c_a_im": c_a_im,
        "c_b_re": c_b_re, "c_b_im": c_b_im, "c_c_re": c_c_re, "c_c_im": c_c_im,
        "c_d": c_d, "c_w_glu": c_w_glu, "c_b_glu": c_b_glu, "c_w_out": c_w_out,
    }


def reference(x, norm_g,
              a_w_in, a_conv_w, a_conv_b, a_ln_g, a_ln_b, a_w_out,
              b_w_in, b_f_bias, b_q_norm, b_k_norm, b_w_out,
              c_w_in, c_log_dt, c_a_re, c_a_im, c_b_re, c_b_im, c_c_re, c_c_im,
              c_d, c_w_glu, c_b_glu, c_w_out):
    h = x
    for layer in range(DEPTH):
        kind = layer % N_MIXERS
        j = layer // N_MIXERS
        hn = rms_norm(h, norm_g[layer])
        if kind == 0:
            y = conv_mixer(hn, a_w_in[j], a_conv_w[j], a_conv_b[j], a_ln_g[j], a_ln_b[j], a_w_out[j])
        elif kind == 1:
            y = fox_mixer(hn, b_w_in[j], b_f_bias[j], b_q_norm[j], b_k_norm[j], b_w_out[j])
        else:
            y = ssm_mixer(hn, c_w_in[j], c_log_dt[j], c_a_re[j], c_a_im[j], c_b_re[j], c_b_im[j],
                          c_c_re[j], c_c_im[j], c_d[j], c_w_glu[j], c_b_glu[j], c_w_out[j])
        h = h + y
    return h
```

```python
import functools
import math

import jax
import jax.numpy as jnp
from jax import lax
from jax.experimental import pallas as pl
from jax.experimental.pallas import tpu as pltpu

F32 = jnp.float32
BF16 = jnp.bfloat16

RMS_EPS = 1e-6
LN_EPS = 1e-5
CONV_K = 31
HEAD_DIM = 64
SSM_GROUP = 16
SSM_STATE = 64

LANES = 128
SUBLANES = 8
MXU_DIM = 256
VMEM_LIMIT_BYTES = 56 * 1024 * 1024

ROW_TILE = 512
ATTN_TILE = 256
SSM_STEPS = 32

CONV_HALO = 32
CONV_CHUNK = 32


def _rms(x, g):
    ms = jnp.mean(x * x, axis=-1, keepdims=True)
    return x * lax.rsqrt(ms + RMS_EPS) * g


def _silu(x):
    return x * jax.nn.sigmoid(x)


def _const_spec(shape):
    zeros = (0,) * len(shape)
    return pl.BlockSpec(shape, lambda *_: zeros)


def _params(sem):
    return pltpu.CompilerParams(dimension_semantics=sem, vmem_limit_bytes=VMEM_LIMIT_BYTES)


def _conv_kernel(h_ref, g_ref, win_ref, cw_ref, cb_ref, lng_ref, lnb_ref, wout_ref, o_ref,
                 ubuf_ref, v_ref, *, tm, width):
    i = pl.program_id(1)

    @pl.when(i == 0)
    def _():
        ubuf_ref[0:CONV_HALO, :] = jnp.zeros((CONV_HALO, width), F32)

    h = h_ref[...]
    hn = _rms(h, g_ref[...]).astype(BF16)
    z = jnp.dot(hn, win_ref[...], preferred_element_type=F32)
    ubuf_ref[CONV_HALO:CONV_HALO + tm, :] = z[:, :width] * jax.nn.sigmoid(z[:, width:2 * width])

    cb = cb_ref[...]
    lng = lng_ref[...]
    lnb = lnb_ref[...]
    base = CONV_HALO - (CONV_K - 1)
    for r0 in range(0, tm, CONV_CHUNK):
        acc = jnp.broadcast_to(cb, (CONV_CHUNK, width))
        for k in range(CONV_K):
            acc = acc + cw_ref[k:k + 1, :] * ubuf_ref[base + r0 + k:base + r0 + k + CONV_CHUNK, :]
        mu = jnp.mean(acc, axis=-1, keepdims=True)
        cen = acc - mu
        var = jnp.mean(cen * cen, axis=-1, keepdims=True)
        y = cen * lax.rsqrt(var + LN_EPS) * lng + lnb
        gate = z[r0:r0 + CONV_CHUNK, 2 * width:]
        v_ref[r0:r0 + CONV_CHUNK, :] = (_silu(y) * _silu(gate)).astype(BF16)

    ubuf_ref[0:CONV_HALO, :] = ubuf_ref[tm:tm + CONV_HALO, :]
    o_ref[...] = h + jnp.dot(v_ref[...], wout_ref[...], preferred_element_type=F32)


def _row_spec(tm, d, nt, seq_major):
    if seq_major:
        return pl.BlockSpec((tm, d), lambda b, i: (i, b))
    return pl.BlockSpec((tm, d), lambda b, i: (b * nt + i, 0))


def _conv_layer(h2d, bsz, seq, seq_major_in, seq_major_out, g, w_in, conv_w, conv_b, ln_g, ln_b, w_out, tm):
    d = g.shape[-1]
    width = conv_w.shape[-1]
    nt = seq // tm
    out_shape = (seq, bsz * d) if seq_major_out else (bsz * seq, d)
    kern = functools.partial(_conv_kernel, tm=tm, width=width)
    return pl.pallas_call(
        kern,
        grid=(bsz, nt),
        in_specs=[
            _row_spec(tm, d, nt, seq_major_in),
            _const_spec((1, d)),
            _const_spec((d, 3 * width)),
            _const_spec((CONV_K, width)),
            _const_spec((1, width)),
            _const_spec((1, width)),
            _const_spec((1, width)),
            _const_spec((width, d)),
        ],
        out_specs=_row_spec(tm, d, nt, seq_major_out),
        out_shape=jax.ShapeDtypeStruct(out_shape, F32),
        scratch_shapes=[pltpu.VMEM((tm + CONV_HALO, width), F32), pltpu.VMEM((tm, width), BF16)],
        compiler_params=_params(("arbitrary", "arbitrary")),
        name="conv_layer",
    )(h2d, g.reshape(1, d), w_in.astype(BF16), conv_w, conv_b.reshape(1, width),
      ln_g.reshape(1, width), ln_b.reshape(1, width), w_out.astype(BF16))


def _split3(x):
    p1 = x.astype(BF16)
    r1 = x - p1.astype(F32)
    p2 = r1.astype(BF16)
    r2 = r1 - p2.astype(F32)
    return p1, p2, r2.astype(BF16)


def _fox_proj_kernel(h_ref, g_ref, wq_ref, wk_ref, wv_ref, wg_ref, wf_ref, fb_ref, qg_ref, kg_ref,
                     seg_ref, tri_ref, q_ref, k_ref, v_ref, gate_ref, cum_ref, carry_ref, *, tm, width):
    i = pl.program_id(1)

    @pl.when(i == 0)
    def _():
        carry_ref[...] = jnp.zeros(carry_ref.shape, F32)

    hn = _rms(h_ref[...], g_ref[...]).astype(BF16)
    seg = seg_ref[...]

    def head_norm(x, gain):
        x2 = x * x
        hi = x2.astype(BF16)
        lo = (x2 - hi.astype(F32)).astype(BF16)
        parts = []
        for s in range(width // MXU_DIM):
            sl = slice(MXU_DIM * s, MXU_DIM * (s + 1))
            parts.append(jnp.dot(hi[:, sl], seg, preferred_element_type=F32)
                         + jnp.dot(lo[:, sl], seg, preferred_element_type=F32))
        ss = jnp.concatenate(parts, axis=-1)
        return x * lax.rsqrt(ss * (1.0 / HEAD_DIM) + RMS_EPS) * gain

    zq = jnp.dot(hn, wq_ref[...], preferred_element_type=F32)
    q_ref[...] = (head_norm(zq, qg_ref[...]) * (1.0 / math.sqrt(HEAD_DIM))).astype(BF16)
    zk = jnp.dot(hn, wk_ref[...], preferred_element_type=F32)
    k_ref[...] = head_norm(zk, kg_ref[...]).astype(BF16)
    v_ref[...] = jnp.dot(hn, wv_ref[...], preferred_element_type=F32).astype(BF16)
    gate_ref[...] = jnp.dot(hn, wg_ref[...], preferred_element_type=F32)

    zf = jnp.dot(hn, wf_ref[...], preferred_element_type=F32) + fb_ref[...]
    logf = jnp.minimum(zf, 0.0) - jnp.log1p(jnp.exp(-jnp.abs(zf)))
    tri = tri_ref[...]
    p1, p2, p3 = _split3(logf)
    local = (jnp.dot(tri, p1, preferred_element_type=F32)
             + jnp.dot(tri, p2, preferred_element_type=F32)
             + jnp.dot(tri, p3, preferred_element_type=F32))
    cum = local + carry_ref[0:1, :]
    cum_ref[...] = cum
    carry_ref[...] = jnp.broadcast_to(cum[tm - 1:tm, :], carry_ref.shape)


def _fox_proj(h2d, bsz, seq, g, w_in, f_bias, q_g, k_g, tm):
    d = g.shape[-1]
    width = (w_in.shape[-1] // (4 * HEAD_DIM + 1)) * HEAD_DIM
    heads = width // HEAD_DIM
    nt = seq // tm
    wb = w_in.astype(BF16)
    wq, wk, wv, wg = (wb[:, n * width:(n + 1) * width] for n in range(4))
    wf = jnp.pad(wb[:, 4 * width:], ((0, 0), (0, LANES - heads)))
    fb = jnp.pad(f_bias, (0, LANES - heads)).reshape(1, LANES)
    seg_id = jnp.arange(MXU_DIM) // HEAD_DIM
    seg = (seg_id[:, None] == seg_id[None, :]).astype(BF16)
    tri = (jnp.arange(tm)[:, None] >= jnp.arange(tm)[None, :]).astype(BF16)
    row = lambda n: pl.BlockSpec((tm, n), lambda b, i: (b * nt + i, 0))
    kern = functools.partial(_fox_proj_kernel, tm=tm, width=width)
    m = bsz * seq
    return pl.pallas_call(
        kern,
        grid=(bsz, nt),
        in_specs=[row(d), _const_spec((1, d))] + [_const_spec((d, width))] * 4
        + [_const_spec((d, LANES)), _const_spec((1, LANES)), _const_spec((1, width)), _const_spec((1, width)),
           _const_spec((MXU_DIM, MXU_DIM)), _const_spec((tm, tm))],
        out_specs=[row(width), row(width), row(width), row(width), row(LANES)],
        out_shape=[jax.ShapeDtypeStruct((m, width), BF16)] * 3
        + [jax.ShapeDtypeStruct((m, width), F32), jax.ShapeDtypeStruct((m, LANES), F32)],
        scratch_shapes=[pltpu.VMEM((SUBLANES, LANES), F32)],
        compiler_params=_params(("arbitrary", "arbitrary")),
        name="fox_proj",
    )(h2d, g.reshape(1, d), wq, wk, wv, wg, wf, fb,
      jnp.tile(q_g, heads).reshape(1, width), jnp.tile(k_g, heads).reshape(1, width), seg, tri)


def _attn_kernel(q_ref, k_ref, v_ref, ck_ref, gate_ref, o_ref, *, tq):
    i = pl.program_id(2)
    lane = lax.broadcasted_iota(jnp.int32, (tq, LANES), 1)
    causal = (lax.broadcasted_iota(jnp.int32, (tq, tq), 0) >= lax.broadcasted_iota(jnp.int32, (tq, tq), 1))
    q = q_ref[...]
    nt_dims = (((1,), (1,)), ((), ()))
    outs = []
    for e in range(LANES // HEAD_DIM):
        in_head = (lane >= HEAD_DIM * e) & (lane < HEAD_DIM * (e + 1))
        qh = jnp.where(in_head, q, jnp.zeros_like(q))

        def block(j, carry, masked, qh=qh, e=e):
            m, l, acc = carry
            rows = pl.ds(pl.multiple_of(j * tq, tq), tq)
            s = lax.dot_general(qh, k_ref[rows, :], nt_dims, preferred_element_type=F32)
            s = s - ck_ref[j, e:e + 1, :]
            if masked:
                s = jnp.where(causal, s, -jnp.inf)
            m_new = jnp.maximum(m, jnp.max(s, axis=-1, keepdims=True))
            alpha = jnp.exp(m - m_new)
            p = jnp.exp(s - m_new)
            l = alpha * l + jnp.sum(p, axis=-1, keepdims=True)
            acc = alpha * acc + jnp.dot(p.astype(BF16), v_ref[rows, :], preferred_element_type=F32)
            return m_new, l, acc

        init = (jnp.full((tq, 1), -jnp.inf, F32), jnp.zeros((tq, 1), F32), jnp.zeros((tq, LANES), F32))
        carry = lax.fori_loop(0, i, functools.partial(block, masked=False), init)
        _, l, acc = block(i, carry, True)
        outs.append(acc / l)
    o = jnp.where(lane < HEAD_DIM, outs[0], outs[1])
    o_ref[...] = (o * _silu(gate_ref[...])).astype(BF16)


def _fox_attn(q, k, v, cum, gate, bsz, seq, tq):
    m, width = q.shape
    heads = width // HEAD_DIM
    pairs = width // LANES
    nq = seq // tq
    ck = cum[:, :heads].reshape(bsz, nq, tq, pairs, 2).transpose(0, 3, 1, 4, 2)
    qspec = pl.BlockSpec((tq, LANES), lambda b, p, i: (b * nq + i, p))
    kvspec = pl.BlockSpec((seq, LANES), lambda b, p, i: (b, p))
    return pl.pallas_call(
        functools.partial(_attn_kernel, tq=tq),
        grid=(bsz, pairs, nq),
        in_specs=[qspec, kvspec, kvspec,
                  pl.BlockSpec((None, None, nq, 2, tq), lambda b, p, i: (b, p, 0, 0, 0)),
                  qspec],
        out_specs=qspec,
        out_shape=jax.ShapeDtypeStruct((m, width), BF16),
        compiler_params=_params(("arbitrary", "arbitrary", "arbitrary")),
        name="fox_attn",
    )(q, k, v, ck, gate)


def _out_proj_kernel(x_ref, w_ref, h_ref, o_ref):
    o_ref[...] = h_ref[...] + jnp.dot(x_ref[...], w_ref[...], preferred_element_type=F32)


def _out_proj(x, w_out, h2d, bsz, seq, seq_major_out, tm):
    width, d = w_out.shape
    nt = seq // tm
    out_shape = (seq, bsz * d) if seq_major_out else (bsz * seq, d)
    return pl.pallas_call(
        _out_proj_kernel,
        grid=(bsz, nt),
        in_specs=[_row_spec(tm, width, nt, False), _const_spec((width, d)), _row_spec(tm, d, nt, False)],
        out_specs=_row_spec(tm, d, nt, seq_major_out),
        out_shape=jax.ShapeDtypeStruct(out_shape, F32),
        compiler_params=_params(("arbitrary", "arbitrary")),
        name="fox_out",
    )(x, w_out.astype(BF16), h2d)


SSM_CHUNK_GROUPS = LANES // SSM_STATE
SSM_CHUNK_COLS = 2 * LANES
SSM_CHUNK_CH = SSM_CHUNK_GROUPS * SSM_GROUP
SSM_SCAN_CHUNKS = 8


def _ssm_kernel(h_ref, g_ref, win_ref, bmat_ref, ar_ref, ai_ref, cmat_ref, d_ref, wglu_ref, bglu_ref,
                wout_ref, o_ref, bu_ref, state_ref, *, ts, nb, width):
    i = pl.program_id(0)
    rows = ts * nb
    nchunk = width // SSM_CHUNK_CH
    per_slab = LANES // SSM_CHUNK_CH

    @pl.when(i == 0)
    def _():
        state_ref[...] = jnp.zeros(state_ref.shape, F32)

    h = h_ref[...].reshape(rows, width)
    hn = _rms(h, g_ref[...]).astype(BF16)
    z = jnp.dot(hn, win_ref[...], preferred_element_type=F32)
    u = z[:, :width]
    ub = u.astype(BF16)

    for c in range(nchunk):
        slab = c // per_slab
        bu_ref[:, SSM_CHUNK_COLS * c:SSM_CHUNK_COLS * (c + 1)] = jnp.dot(
            ub[:, LANES * slab:LANES * (slab + 1)], bmat_ref[c], preferred_element_type=F32)

    for c0 in range(0, nchunk, SSM_SCAN_CHUNKS):
        chunks = range(c0, c0 + SSM_SCAN_CHUNKS)
        ar = [jnp.broadcast_to(ar_ref[:, LANES * c:LANES * (c + 1)], (nb, LANES)) for c in chunks]
        ai = [jnp.broadcast_to(ai_ref[:, LANES * c:LANES * (c + 1)], (nb, LANES)) for c in chunks]

        def step(t, carry, chunks=chunks, ar=ar, ai=ai):
            r = pl.ds(pl.multiple_of(t * nb, nb), nb)
            new = []
            for n, c in enumerate(chunks):
                xr, xi = carry[2 * n], carry[2 * n + 1]
                re_cols = slice(SSM_CHUNK_COLS * c, SSM_CHUNK_COLS * c + LANES)
                im_cols = slice(SSM_CHUNK_COLS * c + LANES, SSM_CHUNK_COLS * (c + 1))
                nr = ar[n] * xr - ai[n] * xi + bu_ref[r, re_cols]
                ni = ar[n] * xi + ai[n] * xr + bu_ref[r, im_cols]
                bu_ref[r, re_cols] = nr
                bu_ref[r, im_cols] = ni
                new += [nr, ni]
            return tuple(new)

        init = tuple(state_ref[:, LANES * k:LANES * (k + 1)] for k in range(2 * c0, 2 * (c0 + SSM_SCAN_CHUNKS)))
        fin = lax.fori_loop(0, ts, step, init)
        for n, k in enumerate(range(2 * c0, 2 * (c0 + SSM_SCAN_CHUNKS))):
            state_ref[:, LANES * k:LANES * (k + 1)] = fin[n]

    slabs = []
    for s in range(width // LANES):
        acc = None
        for c in range(per_slab * s, per_slab * (s + 1)):
            xc = bu_ref[:, SSM_CHUNK_COLS * c:SSM_CHUNK_COLS * (c + 1)].astype(BF16)
            part = jnp.dot(xc, cmat_ref[c], preferred_element_type=F32)
            acc = part if acc is None else acc + part
        slabs.append(acc)
    y = jnp.concatenate(slabs, axis=-1) + d_ref[...] * u
    gl = jax.nn.gelu(y)
    y = gl * jax.nn.sigmoid(jnp.dot(gl.astype(BF16), wglu_ref[...], preferred_element_type=F32) + bglu_ref[...])
    v = (y * _silu(z[:, width:])).astype(BF16)
    out = h + jnp.dot(v, wout_ref[...], preferred_element_type=F32)
    o_ref[...] = out.reshape(ts, nb, width)


def _ssm_discretize(log_dt, a_re, a_im, b_re, b_im, c_re, c_im):
    groups, state = a_re.shape
    nchunk = groups // SSM_CHUNK_GROUPS
    per_slab = LANES // SSM_CHUNK_CH
    dt = jnp.exp(log_dt)[:, None]
    mag = jnp.exp(a_re * dt)
    ang = a_im * dt
    abar_re = mag * jnp.cos(ang)
    abar_im = mag * jnp.sin(ang)
    den = a_re * a_re + a_im * a_im
    nr = abar_re - 1.0
    ni = abar_im
    zr = ((nr * a_re + ni * a_im) / den)[..., None]
    zi = ((ni * a_re - nr * a_im) / den)[..., None]
    bb_re = zr * b_re - zi * b_im
    bb_im = zr * b_im + zi * b_re
    eye = jnp.eye(SSM_CHUNK_GROUPS, dtype=F32)
    slot = jax.nn.one_hot(jnp.arange(nchunk) % per_slab, per_slab, dtype=F32)

    def b_block(bb):
        x = bb.transpose(0, 2, 1).reshape(nchunk, SSM_CHUNK_GROUPS, SSM_GROUP, state)
        return jnp.einsum("cgip,gh->cgihp", x, eye).reshape(nchunk, SSM_CHUNK_CH, LANES)

    bblk = jnp.concatenate([b_block(bb_re), b_block(bb_im)], axis=-1)
    bmat = jnp.einsum("cq,crn->cqrn", slot, bblk).reshape(nchunk, LANES, SSM_CHUNK_COLS)

    def c_block(cc):
        x = cc.transpose(0, 2, 1).reshape(nchunk, SSM_CHUNK_GROUPS, state, SSM_GROUP)
        return jnp.einsum("cgpo,gh->cgpho", x, eye).reshape(nchunk, LANES, SSM_CHUNK_CH)

    cblk = jnp.concatenate([c_block(c_re), -c_block(c_im)], axis=1)
    cmat = jnp.einsum("cq,ckr->ckqr", slot, cblk).reshape(nchunk, SSM_CHUNK_COLS, LANES)
    return (bmat.astype(BF16), abar_re.reshape(1, groups * state), abar_im.reshape(1, groups * state),
            cmat.astype(BF16))


def _ssm_layer(h3d, g, w_in, log_dt, a_re, a_im, b_re, b_im, c_re, c_im, d_skip, w_glu, b_glu, w_out, ts):
    seq, nb, d = h3d.shape
    width = d_skip.shape[-1]
    groups, state = a_re.shape
    assert nb == SUBLANES and state == SSM_STATE and width == groups * SSM_GROUP
    nchunk = groups // SSM_CHUNK_GROUPS
    assert nchunk % SSM_SCAN_CHUNKS == 0
    bmat, ar, ai, cmat = _ssm_discretize(log_dt, a_re, a_im, b_re, b_im, c_re, c_im)
    ncols = groups * state
    blk = pl.BlockSpec((ts, nb, d), lambda i: (i, 0, 0))
    kern = functools.partial(_ssm_kernel, ts=ts, nb=nb, width=width)
    return pl.pallas_call(
        kern,
        grid=(seq // ts,),
        in_specs=[blk, _const_spec((1, d)), _const_spec((d, 2 * width)),
                  _const_spec((nchunk, LANES, SSM_CHUNK_COLS)), _const_spec((1, ncols)), _const_spec((1, ncols)),
                  _const_spec((nchunk, SSM_CHUNK_COLS, LANES)), _const_spec((1, width)),
                  _const_spec((width, width)), _const_spec((1, width)), _const_spec((width, d))],
        out_specs=blk,
        out_shape=jax.ShapeDtypeStruct((seq, nb, d), F32),
        scratch_shapes=[pltpu.VMEM((ts * nb, 2 * ncols), F32), pltpu.VMEM((nb, 2 * ncols), F32)],
        compiler_params=_params(("arbitrary",)),
        name="ssm_layer",
    )(h3d, g.reshape(1, d), w_in.astype(BF16), bmat, ar, ai, cmat, d_skip.reshape(1, width),
      w_glu.astype(BF16), b_glu.reshape(1, width), w_out.astype(BF16))


def kernel(x, norm_g, a_w_in, a_conv_w, a_conv_b, a_ln_g, a_ln_b, a_w_out, b_w_in, b_f_bias, b_q_norm,
           b_k_norm, b_w_out, c_w_in, c_log_dt, c_a_re, c_a_im, c_b_re, c_b_im, c_c_re, c_c_im, c_d,
           c_w_glu, c_b_glu, c_w_out):
    bsz, seq, d = x.shape
    depth = norm_g.shape[0]
    tm = min(ROW_TILE, seq)
    tq = min(ATTN_TILE, seq)
    ts = min(SSM_STEPS, seq)
    h = x.reshape(bsz * seq, d)
    seq_major = False
    for layer in range(depth):
        kind, j = layer % 3, layer // 3
        next_major = (layer + 1 < depth) and ((layer + 1) % 3 == 2)
        if kind == 0:
            h = _conv_layer(h, bsz, seq, seq_major, next_major, norm_g[layer], a_w_in[j], a_conv_w[j],
                            a_conv_b[j], a_ln_g[j], a_ln_b[j], a_w_out[j], tm)
        elif kind == 1:
            assert not seq_major
            q, k, v, gate, cum = _fox_proj(h, bsz, seq, norm_g[layer], b_w_in[j], b_f_bias[j],
                                           b_q_norm[j], b_k_norm[j], tm)
            og = _fox_attn(q, k, v, cum, gate, bsz, seq, tq)
            h = _out_proj(og, b_w_out[j], h, bsz, seq, next_major, tm)
        else:
            assert seq_major
            h3 = _ssm_layer(h.reshape(seq, bsz, d), norm_g[layer], c_w_in[j], c_log_dt[j], c_a_re[j],
                            c_a_im[j], c_b_re[j], c_b_im[j], c_c_re[j], c_c_im[j], c_d[j], c_w_glu[j],
                            c_b_glu[j], c_w_out[j], ts)
            h = h3.reshape(seq, bsz * d)
            next_major = True
        seq_major = next_major
    if seq_major:
        h = h.reshape(seq, bsz, d).transpose(1, 0, 2)
    return h.reshape(bsz, seq, d)
```

```python
import functools
import math

import jax
import jax.numpy as jnp
from jax import lax
from jax.experimental import pallas as pl
from jax.experimental.pallas import tpu as pltpu

F32 = jnp.float32
BF16 = jnp.bfloat16

RMS_EPS = 1e-6
LN_EPS = 1e-5
CONV_K = 31
HEAD_DIM = 64
SSM_GROUP = 16
SSM_STATE = 64

LANES = 128
SUBLANES = 8
MXU_DIM = 256
VMEM_LIMIT_BYTES = 56 * 1024 * 1024

ROW_TILE = 512
ATTN_TILE = 256
SSM_STEPS = 32

ATTN_PAIRS = 4
HEADS_PER_SLAB = LANES // HEAD_DIM
DECAY_PARTS = 3
ONES_ROWS = 16
LOG2E = math.log2(math.e)
NT_DIMS = (((1,), (1,)), ((), ()))

CONV_HALO = 32
CONV_CHUNK = 32
CONV_ROWS = 64
CONV_STRIP = 256


def _rms(x, g):
    ms = jnp.mean(x * x, axis=-1, keepdims=True)
    return x * lax.rsqrt(ms + RMS_EPS) * g


def _silu(x):
    return x * jax.nn.sigmoid(x)


def _const_spec(shape):
    zeros = (0,) * len(shape)
    return pl.BlockSpec(shape, lambda *_: zeros)


def _params(sem):
    return pltpu.CompilerParams(dimension_semantics=sem, vmem_limit_bytes=VMEM_LIMIT_BYTES)


def _conv_kernel(h_ref, g_ref, win_ref, cw_ref, cb_ref, lng_ref, lnb_ref, wout_ref, o_ref,
                 ubuf_ref, c_ref, v_ref, *, tm, width):
    i = pl.program_id(1)

    @pl.when(i == 0)
    def _():
        ubuf_ref[0:CONV_HALO, :] = jnp.zeros((CONV_HALO, width), F32)

    h = h_ref[...]
    hn = _rms(h, g_ref[...]).astype(BF16)
    z = jnp.dot(hn, win_ref[...], preferred_element_type=F32)
    ubuf_ref[CONV_HALO:CONV_HALO + tm, :] = z[:, :width] * jax.nn.sigmoid(z[:, width:2 * width])

    base = CONV_HALO - (CONV_K - 1)
    for r0 in range(0, tm, CONV_ROWS):
        for c0 in range(0, width, CONV_STRIP):
            cols = slice(c0, c0 + CONV_STRIP)
            acc = jnp.broadcast_to(cb_ref[:, cols], (CONV_ROWS, CONV_STRIP))
            for r in range(SUBLANES):
                nrows = CONV_ROWS + (SUBLANES if r else 0)
                part = None
                for a in range((base + CONV_K - 1) // SUBLANES + 1):
                    k = SUBLANES * a + r - base
                    if 0 <= k < CONV_K:
                        row = r0 + SUBLANES * a
                        term = cw_ref[k:k + 1, cols] * ubuf_ref[row:row + nrows, cols]
                        part = term if part is None else part + term
                acc = acc + part[r:r + CONV_ROWS, :]
            c_ref[r0:r0 + CONV_ROWS, cols] = acc

    lng = lng_ref[...]
    lnb = lnb_ref[...]
    for r0 in range(0, tm, CONV_CHUNK):
        acc = c_ref[r0:r0 + CONV_CHUNK, :]
        mu = jnp.mean(acc, axis=-1, keepdims=True)
        cen = acc - mu
        var = jnp.mean(cen * cen, axis=-1, keepdims=True)
        y = cen * lax.rsqrt(var + LN_EPS) * lng + lnb
        gate = z[r0:r0 + CONV_CHUNK, 2 * width:]
        v_ref[r0:r0 + CONV_CHUNK, :] = (_silu(y) * _silu(gate)).astype(BF16)

    ubuf_ref[0:CONV_HALO, :] = ubuf_ref[tm:tm + CONV_HALO, :]
    o_ref[...] = h + jnp.dot(v_ref[...], wout_ref[...], preferred_element_type=F32)


def _row_spec(tm, d, nt, seq_major):
    if seq_major:
        return pl.BlockSpec((tm, d), lambda b, i: (i, b))
    return pl.BlockSpec((tm, d), lambda b, i: (b * nt + i, 0))


def _conv_layer(h2d, bsz, seq, seq_major_in, seq_major_out, g, w_in, conv_w, conv_b, ln_g, ln_b, w_out, tm):
    d = g.shape[-1]
    width = conv_w.shape[-1]
    nt = seq // tm
    out_shape = (seq, bsz * d) if seq_major_out else (bsz * seq, d)
    kern = functools.partial(_conv_kernel, tm=tm, width=width)
    return pl.pallas_call(
        kern,
        grid=(bsz, nt),
        in_specs=[
            _row_spec(tm, d, nt, seq_major_in),
            _const_spec((1, d)),
            _const_spec((d, 3 * width)),
            _const_spec((CONV_K, width)),
            _const_spec((1, width)),
            _const_spec((1, width)),
            _const_spec((1, width)),
            _const_spec((width, d)),
        ],
        out_specs=_row_spec(tm, d, nt, seq_major_out),
        out_shape=jax.ShapeDtypeStruct(out_shape, F32),
        scratch_shapes=[pltpu.VMEM((tm + CONV_HALO, width), F32), pltpu.VMEM((tm, width), F32),
                        pltpu.VMEM((tm, width), BF16)],
        compiler_params=_params(("arbitrary", "arbitrary")),
        name="conv_layer",
    )(h2d, g.reshape(1, d), w_in.astype(BF16), conv_w, conv_b.reshape(1, width),
      ln_g.reshape(1, width), ln_b.reshape(1, width), w_out.astype(BF16))


def _split3(x):
    p1 = x.astype(BF16)
    r1 = x - p1.astype(F32)
    p2 = r1.astype(BF16)
    r2 = r1 - p2.astype(F32)
    return p1, p2, r2.astype(BF16)


def _fox_proj_kernel(h_ref, g_ref, wq_ref, wk_ref, wvt_ref, wg_ref, wf_ref, fb_ref, qg_ref, kg_ref,
                     seg_ref, tri_ref, place_ref, q_ref, k_ref, kc_ref, vt_ref, gate_ref, carry_ref,
                     *, tm, width):
    i = pl.program_id(1)
    heads = width // HEAD_DIM

    @pl.when(i == 0)
    def _():
        carry_ref[...] = jnp.zeros(carry_ref.shape, F32)

    hn = _rms(h_ref[...], g_ref[...]).astype(BF16)
    seg = seg_ref[...]

    def head_norm(x, gain):
        x2 = x * x
        hi = x2.astype(BF16)
        lo = (x2 - hi.astype(F32)).astype(BF16)
        parts = []
        for s in range(width // MXU_DIM):
            sl = slice(MXU_DIM * s, MXU_DIM * (s + 1))
            parts.append(jnp.dot(hi[:, sl], seg, preferred_element_type=F32)
                         + jnp.dot(lo[:, sl], seg, preferred_element_type=F32))
        ss = jnp.concatenate(parts, axis=-1)
        return x * lax.rsqrt(ss * (1.0 / HEAD_DIM) + RMS_EPS) * gain

    zq = jnp.dot(hn, wq_ref[...], preferred_element_type=F32)
    q_ref[...] = (head_norm(zq, qg_ref[...]) * (LOG2E / math.sqrt(HEAD_DIM))).astype(BF16)
    zk = jnp.dot(hn, wk_ref[...], preferred_element_type=F32)
    k_ref[...] = head_norm(zk, kg_ref[...]).astype(BF16)
    vt_ref[...] = lax.dot_general(wvt_ref[...], hn, NT_DIMS, preferred_element_type=F32).astype(BF16)
    gate_ref[...] = jnp.dot(hn, wg_ref[...], preferred_element_type=F32)

    zf = jnp.dot(hn, wf_ref[...], preferred_element_type=F32) + fb_ref[...]
    logf = jnp.minimum(zf, 0.0) - jnp.log1p(jnp.exp(-jnp.abs(zf)))
    tri = tri_ref[...]
    p1, p2, p3 = _split3(logf)
    local = (jnp.dot(tri, p1, preferred_element_type=F32)
             + jnp.dot(tri, p2, preferred_element_type=F32)
             + jnp.dot(tri, p3, preferred_element_type=F32))
    cum = local + carry_ref[0:1, :]
    carry_ref[...] = jnp.broadcast_to(cum[tm - 1:tm, :], carry_ref.shape)
    d1, d2, d3 = _split3(cum * (-LOG2E))
    lane = lax.broadcasted_iota(jnp.int32, (tm, LANES), 1)
    parts = jnp.where(lane < heads, d1, jnp.where(lane < 2 * heads, d2, d3))
    kc_ref[...] = jnp.dot(parts, place_ref[...], preferred_element_type=F32).astype(BF16)


def _fox_proj(h2d, bsz, seq, g, w_in, f_bias, q_g, k_g, tm):
    d = g.shape[-1]
    width = (w_in.shape[-1] // (4 * HEAD_DIM + 1)) * HEAD_DIM
    heads = width // HEAD_DIM
    assert DECAY_PARTS * heads <= LANES
    nt = seq // tm
    wb = w_in.astype(BF16)
    wq, wk, wv, wg = (wb[:, n * width:(n + 1) * width] for n in range(4))
    wf = jnp.pad(jnp.tile(wb[:, 4 * width:], (1, DECAY_PARTS)), ((0, 0), (0, LANES - DECAY_PARTS * heads)))
    fb = jnp.pad(jnp.tile(f_bias, DECAY_PARTS), (0, LANES - DECAY_PARTS * heads)).reshape(1, LANES)
    seg_id = jnp.arange(MXU_DIM) // HEAD_DIM
    seg = (seg_id[:, None] == seg_id[None, :]).astype(BF16)
    tri = (jnp.arange(tm)[:, None] >= jnp.arange(tm)[None, :]).astype(BF16)
    src = jnp.arange(LANES)
    part, head = src // heads, src % heads
    dst = (head // HEADS_PER_SLAB) * LANES + DECAY_PARTS * (head % HEADS_PER_SLAB) + part
    place = ((dst[:, None] == jnp.arange(width)[None, :]) & (src[:, None] < DECAY_PARTS * heads)).astype(BF16)
    row = lambda n: pl.BlockSpec((tm, n), lambda b, i: (b * nt + i, 0))
    kern = functools.partial(_fox_proj_kernel, tm=tm, width=width)
    m = bsz * seq
    return pl.pallas_call(
        kern,
        grid=(bsz, nt),
        in_specs=[row(d), _const_spec((1, d)), _const_spec((d, width)), _const_spec((d, width)),
                  _const_spec((width, d)), _const_spec((d, width)),
                  _const_spec((d, LANES)), _const_spec((1, LANES)), _const_spec((1, width)), _const_spec((1, width)),
                  _const_spec((MXU_DIM, MXU_DIM)), _const_spec((tm, tm)), _const_spec((LANES, width))],
        out_specs=[row(width), row(width), row(width),
                   pl.BlockSpec((None, None, width, tm), lambda b, i: (b, i, 0, 0)), row(width)],
        out_shape=[jax.ShapeDtypeStruct((m, width), BF16)] * 3
        + [jax.ShapeDtypeStruct((bsz, nt, width, tm), BF16), jax.ShapeDtypeStruct((m, width), F32)],
        scratch_shapes=[pltpu.VMEM((SUBLANES, LANES), F32)],
        compiler_params=_params(("arbitrary", "arbitrary")),
        name="fox_proj",
    )(h2d, g.reshape(1, d), wq, wk, wv.T, wg, wf, fb,
      jnp.tile(q_g, heads).reshape(1, width), jnp.tile(k_g, heads).reshape(1, width), seg, tri, place)


def _attn_kernel(q_ref, k_ref, kc_ref, vt_ref, gate_ref, o_ref, qt_ref, st_ref, acc_ref, *, tq, npairs):
    i = pl.program_id(2)
    nchains = HEADS_PER_SLAB * npairs
    lane = lax.broadcasted_iota(jnp.int32, (tq, LANES), 1)
    key_minus_query = (lax.broadcasted_iota(jnp.int32, (tq, tq), 0) - lax.broadcasted_iota(jnp.int32, (tq, tq), 1))
    ones_rows = jnp.ones((ONES_ROWS, tq), BF16)

    def band(lo, hi):
        return jnp.where(lane < hi, 1.0, 0.0) * jnp.where(lane < lo, 0.0, 1.0)

    for p in range(npairs):
        q = q_ref[:, LANES * p:LANES * (p + 1)].astype(F32)
        for e in range(HEADS_PER_SLAB):
            q_aug = jnp.concatenate(
                [q * band(HEAD_DIM * e, HEAD_DIM * (e + 1)), band(DECAY_PARTS * e, DECAY_PARTS * (e + 1))], axis=-1)
            qt_ref[HEADS_PER_SLAB * p + e] = q_aug.T.astype(BF16)
    acc_ref[...] = jnp.zeros(acc_ref.shape, F32)

    def scores(j, mx):
        rows = pl.ds(pl.multiple_of(j * tq, tq), tq)
        visible = key_minus_query <= (i - j) * tq
        new = []
        for p in range(npairs):
            slab = slice(LANES * p, LANES * (p + 1))
            k_aug = jnp.concatenate([k_ref[rows, slab], kc_ref[rows, slab]], axis=-1)
            for e in range(HEADS_PER_SLAB):
                c = HEADS_PER_SLAB * p + e
                st = jnp.dot(k_aug, qt_ref[c], preferred_element_type=F32)
                st = jnp.where(visible, st, -jnp.inf)
                st_ref[c, j] = st
                new.append(jnp.maximum(mx[c], jnp.max(st.reshape(tq // SUBLANES, SUBLANES, tq), axis=0)))
        return tuple(new)

    mx = lax.fori_loop(0, i + 1, scores, tuple(jnp.full((SUBLANES, tq), -jnp.inf, F32) for _ in range(nchains)))
    col_max = [jnp.max(m, axis=0, keepdims=True) for m in mx]

    def accumulate(j, carry):
        for p in range(npairs):
            v_aug = jnp.concatenate([vt_ref[j, LANES * p:LANES * (p + 1), :], ones_rows], axis=0)
            for e in range(HEADS_PER_SLAB):
                c = HEADS_PER_SLAB * p + e
                pt = jnp.exp2(st_ref[c, j] - col_max[c]).astype(BF16)
                acc_ref[c] += jnp.dot(v_aug, pt, preferred_element_type=F32)
        return carry

    lax.fori_loop(0, i + 1, accumulate, 0)

    for p in range(npairs):
        halves = []
        for e in range(HEADS_PER_SLAB):
            acc = acc_ref[HEADS_PER_SLAB * p + e]
            halves.append(acc[HEAD_DIM * e:HEAD_DIM * (e + 1), :] / acc[LANES:LANES + 1, :])
        o = jnp.concatenate(halves, axis=0).T
        slab = slice(LANES * p, LANES * (p + 1))
        o_ref[:, slab] = (o * _silu(gate_ref[:, slab])).astype(BF16)


def _fox_attn(q, k, kc, vt, gate, bsz, seq, tq):
    m, width = q.shape
    nq = seq // tq
    npairs = min(ATTN_PAIRS, width // LANES)
    nchains = HEADS_PER_SLAB * npairs
    cols = npairs * LANES
    qspec = pl.BlockSpec((tq, cols), lambda b, p, i: (b * nq + i, p))
    kspec = pl.BlockSpec((seq, cols), lambda b, p, i: (b, p))
    return pl.pallas_call(
        functools.partial(_attn_kernel, tq=tq, npairs=npairs),
        grid=(bsz, width // cols, nq),
        in_specs=[qspec, kspec, kspec,
                  pl.BlockSpec((None, nq, cols, tq), lambda b, p, i: (b, 0, p, 0)),
                  qspec],
        out_specs=qspec,
        out_shape=jax.ShapeDtypeStruct((m, width), BF16),
        scratch_shapes=[pltpu.VMEM((nchains, 2 * LANES, tq), BF16),
                        pltpu.VMEM((nchains, nq, tq, tq), F32),
                        pltpu.VMEM((nchains, LANES + ONES_ROWS, tq), F32)],
        compiler_params=_params(("arbitrary", "arbitrary", "arbitrary")),
        name="fox_attn",
    )(q, k, kc, vt, gate)


def _out_proj_kernel(x_ref, w_ref, h_ref, o_ref):
    o_ref[...] = h_ref[...] + jnp.dot(x_ref[...], w_ref[...], preferred_element_type=F32)


def _out_proj(x, w_out, h2d, bsz, seq, seq_major_out, tm):
    width, d = w_out.shape
    nt = seq // tm
    out_shape = (seq, bsz * d) if seq_major_out else (bsz * seq, d)
    return pl.pallas_call(
        _out_proj_kernel,
        grid=(bsz, nt),
        in_specs=[_row_spec(tm, width, nt, False), _const_spec((width, d)), _row_spec(tm, d, nt, False)],
        out_specs=_row_spec(tm, d, nt, seq_major_out),
        out_shape=jax.ShapeDtypeStruct(out_shape, F32),
        compiler_params=_params(("arbitrary", "arbitrary")),
        name="fox_out",
    )(x, w_out.astype(BF16), h2d)


SSM_CHUNK_GROUPS = LANES // SSM_STATE
SSM_CHUNK_COLS = 2 * LANES
SSM_CHUNK_CH = SSM_CHUNK_GROUPS * SSM_GROUP
SSM_SCAN_CHUNKS = 8


def _ssm_kernel(h_ref, g_ref, win_ref, bmat_ref, ar_ref, ai_ref, cmat_ref, d_ref, wglu_ref, bglu_ref,
                wout_ref, o_ref, bu_ref, state_ref, *, ts, nb, width):
    i = pl.program_id(0)
    rows = ts * nb
    nchunk = width // SSM_CHUNK_CH
    per_slab = LANES // SSM_CHUNK_CH

    @pl.when(i == 0)
    def _():
        state_ref[...] = jnp.zeros(state_ref.shape, F32)

    h = h_ref[...].reshape(rows, width)
    hn = _rms(h, g_ref[...]).astype(BF16)
    z = jnp.dot(hn, win_ref[...], preferred_element_type=F32)
    u = z[:, :width]
    ub = u.astype(BF16)

    for c in range(nchunk):
        slab = c // per_slab
        bu_ref[:, SSM_CHUNK_COLS * c:SSM_CHUNK_COLS * (c + 1)] = jnp.dot(
            ub[:, LANES * slab:LANES * (slab + 1)], bmat_ref[c], preferred_element_type=F32)

    for c0 in range(0, nchunk, SSM_SCAN_CHUNKS):
        chunks = range(c0, c0 + SSM_SCAN_CHUNKS)
        ar = [jnp.broadcast_to(ar_ref[:, LANES * c:LANES * (c + 1)], (nb, LANES)) for c in chunks]
        ai = [jnp.broadcast_to(ai_ref[:, LANES * c:LANES * (c + 1)], (nb, LANES)) for c in chunks]

        def step(t, carry, chunks=chunks, ar=ar, ai=ai):
            r = pl.ds(pl.multiple_of(t * nb, nb), nb)
            new = []
            for n, c in enumerate(chunks):
                xr, xi = carry[2 * n], carry[2 * n + 1]
                re_cols = slice(SSM_CHUNK_COLS * c, SSM_CHUNK_COLS * c + LANES)
                im_cols = slice(SSM_CHUNK_COLS * c + LANES, SSM_CHUNK_COLS * (c + 1))
                nr = ar[n] * xr - ai[n] * xi + bu_ref[r, re_cols]
                ni = ar[n] * xi + ai[n] * xr + bu_ref[r, im_cols]
                bu_ref[r, re_cols] = nr
                bu_ref[r, im_cols] = ni
                new += [nr, ni]
            return tuple(new)

        init = tuple(state_ref[:, LANES * k:LANES * (k + 1)] for k in range(2 * c0, 2 * (c0 + SSM_SCAN_CHUNKS)))
        fin = lax.fori_loop(0, ts, step, init)
        for n, k in enumerate(range(2 * c0, 2 * (c0 + SSM_SCAN_CHUNKS))):
            state_ref[:, LANES * k:LANES * (k + 1)] = fin[n]

    slabs = []
    for s in range(width // LANES):
        acc = None
        for c in range(per_slab * s, per_slab * (s + 1)):
            xc = bu_ref[:, SSM_CHUNK_COLS * c:SSM_CHUNK_COLS * (c + 1)].astype(BF16)
            part = jnp.dot(xc, cmat_ref[c], preferred_element_type=F32)
            acc = part if acc is None else acc + part
        slabs.append(acc)
    y = jnp.concatenate(slabs, axis=-1) + d_ref[...] * u
    gl = jax.nn.gelu(y)
    y = gl * jax.nn.sigmoid(jnp.dot(gl.astype(BF16), wglu_ref[...], preferred_element_type=F32) + bglu_ref[...])
    v = (y * _silu(z[:, width:])).astype(BF16)
    out = h + jnp.dot(v, wout_ref[...], preferred_element_type=F32)
    o_ref[...] = out.reshape(ts, nb, width)


def _ssm_discretize(log_dt, a_re, a_im, b_re, b_im, c_re, c_im):
    groups, state = a_re.shape
    nchunk = groups // SSM_CHUNK_GROUPS
    per_slab = LANES // SSM_CHUNK_CH
    dt = jnp.exp(log_dt)[:, None]
    mag = jnp.exp(a_re * dt)
    ang = a_im * dt
    abar_re = mag * jnp.cos(ang)
    abar_im = mag * jnp.sin(ang)
    den = a_re * a_re + a_im * a_im
    nr = abar_re - 1.0
    ni = abar_im
    zr = ((nr * a_re + ni * a_im) / den)[..., None]
    zi = ((ni * a_re - nr * a_im) / den)[..., None]
    bb_re = zr * b_re - zi * b_im
    bb_im = zr * b_im + zi * b_re
    eye = jnp.eye(SSM_CHUNK_GROUPS, dtype=F32)
    slot = jax.nn.one_hot(jnp.arange(nchunk) % per_slab, per_slab, dtype=F32)

    def b_block(bb):
        x = bb.transpose(0, 2, 1).reshape(nchunk, SSM_CHUNK_GROUPS, SSM_GROUP, state)
        return jnp.einsum("cgip,gh->cgihp", x, eye).reshape(nchunk, SSM_CHUNK_CH, LANES)

    bblk = jnp.concatenate([b_block(bb_re), b_block(bb_im)], axis=-1)
    bmat = jnp.einsum("cq,crn->cqrn", slot, bblk).reshape(nchunk, LANES, SSM_CHUNK_COLS)

    def c_block(cc):
        x = cc.transpose(0, 2, 1).reshape(nchunk, SSM_CHUNK_GROUPS, state, SSM_GROUP)
        return jnp.einsum("cgpo,gh->cgpho", x, eye).reshape(nchunk, LANES, SSM_CHUNK_CH)

    cblk = jnp.concatenate([c_block(c_re), -c_block(c_im)], axis=1)
    cmat = jnp.einsum("cq,ckr->ckqr", slot, cblk).reshape(nchunk, SSM_CHUNK_COLS, LANES)
    return (bmat.astype(BF16), abar_re.reshape(1, groups * state), abar_im.reshape(1, groups * state),
            cmat.astype(BF16))


def _ssm_layer(h3d, g, w_in, log_dt, a_re, a_im, b_re, b_im, c_re, c_im, d_skip, w_glu, b_glu, w_out, ts):
    seq, nb, d = h3d.shape
    width = d_skip.shape[-1]
    groups, state = a_re.shape
    assert nb == SUBLANES and state == SSM_STATE and width == groups * SSM_GROUP
    nchunk = groups // SSM_CHUNK_GROUPS
    assert nchunk % SSM_SCAN_CHUNKS == 0
    bmat, ar, ai, cmat = _ssm_discretize(log_dt, a_re, a_im, b_re, b_im, c_re, c_im)
    ncols = groups * state
    blk = pl.BlockSpec((ts, nb, d), lambda i: (i, 0, 0))
    kern = functools.partial(_ssm_kernel, ts=ts, nb=nb, width=width)
    return pl.pallas_call(
        kern,
        grid=(seq // ts,),
        in_specs=[blk, _const_spec((1, d)), _const_spec((d, 2 * width)),
                  _const_spec((nchunk, LANES, SSM_CHUNK_COLS)), _const_spec((1, ncols)), _const_spec((1, ncols)),
                  _const_spec((nchunk, SSM_CHUNK_COLS, LANES)), _const_spec((1, width)),
                  _const_spec((width, width)), _const_spec((1, width)), _const_spec((width, d))],
        out_specs=blk,
        out_shape=jax.ShapeDtypeStruct((seq, nb, d), F32),
        scratch_shapes=[pltpu.VMEM((ts * nb, 2 * ncols), F32), pltpu.VMEM((nb, 2 * ncols), F32)],
        compiler_params=_params(("arbitrary",)),
        name="ssm_layer",
    )(h3d, g.reshape(1, d), w_in.astype(BF16), bmat, ar, ai, cmat, d_skip.reshape(1, width),
      w_glu.astype(BF16), b_glu.reshape(1, width), w_out.astype(BF16))


def kernel(x, norm_g, a_w_in, a_conv_w, a_conv_b, a_ln_g, a_ln_b, a_w_out, b_w_in, b_f_bias, b_q_norm,
           b_k_norm, b_w_out, c_w_in, c_log_dt, c_a_re, c_a_im, c_b_re, c_b_im, c_c_re, c_c_im, c_d,
           c_w_glu, c_b_glu, c_w_out):
    bsz, seq, d = x.shape
    depth = norm_g.shape[0]
    tm = min(ROW_TILE, seq)
    tq = min(ATTN_TILE, seq)
    ts = min(SSM_STEPS, seq)
    h = x.reshape(bsz * seq, d)
    seq_major = False
    for layer in range(depth):
        kind, j = layer % 3, layer // 3
        next_major = (layer + 1 < depth) and ((layer + 1) % 3 == 2)
        if kind == 0:
            h = _conv_layer(h, bsz, seq, seq_major, next_major, norm_g[layer], a_w_in[j], a_conv_w[j],
                            a_conv_b[j], a_ln_g[j], a_ln_b[j], a_w_out[j], tm)
        elif kind == 1:
            assert not seq_major
            q, k, kc, vt, gate = _fox_proj(h, bsz, seq, norm_g[layer], b_w_in[j], b_f_bias[j],
                                           b_q_norm[j], b_k_norm[j], tq)
            og = _fox_attn(q, k, kc, vt, gate, bsz, seq, tq)
            h = _out_proj(og, b_w_out[j], h, bsz, seq, next_major, tm)
        else:
            assert seq_major
            h3 = _ssm_layer(h.reshape(seq, bsz, d), norm_g[layer], c_w_in[j], c_log_dt[j], c_a_re[j],
                            c_a_im[j], c_b_re[j], c_b_im[j], c_c_re[j], c_c_im[j], c_d[j], c_w_glu[j],
                            c_b_glu[j], c_w_out[j], ts)
            h = h3.reshape(seq, bsz * d)
            next_major = True
        seq_major = next_major
    if seq_major:
        h = h.reshape(seq, bsz, d).transpose(1, 0, 2)
    return h.reshape(bsz, seq, d)
```

```python
import functools
import math

import jax
import jax.numpy as jnp
from jax import lax
from jax.experimental import pallas as pl
from jax.experimental.pallas import tpu as pltpu

F32 = jnp.float32
BF16 = jnp.bfloat16

RMS_EPS = 1e-6
LN_EPS = 1e-5
CONV_K = 31
HEAD_DIM = 64
SSM_GROUP = 16
SSM_STATE = 64

LANES = 128
SUBLANES = 8
MXU_DIM = 256
VMEM_LIMIT_BYTES = 56 * 1024 * 1024

ROW_TILE = 512
ATTN_TILE = 256
SSM_STEPS = 64

ATTN_PAIRS = 4
ATTN_UNROLL = 2
HEADS_PER_SLAB = LANES // HEAD_DIM
DECAY_PARTS = 3
ONES_ROWS = 16
LOG2E = math.log2(math.e)
NT_DIMS = (((1,), (1,)), ((), ()))

CONV_HALO = 32
CONV_CHUNK = 32
CONV_ROWS = 64
CONV_STRIP = 256
CONV_SUB = 256


def _rms(x, g):
    ms = jnp.mean(x * x, axis=-1, keepdims=True)
    return x * lax.rsqrt(ms + RMS_EPS) * g


def _silu(x):
    return x * jax.nn.sigmoid(x)


def _const_spec(shape):
    zeros = (0,) * len(shape)
    return pl.BlockSpec(shape, lambda *_: zeros, pipeline_mode=pl.Buffered(1))


def _params(sem):
    return pltpu.CompilerParams(dimension_semantics=sem, vmem_limit_bytes=VMEM_LIMIT_BYTES)


def _conv_kernel(h_ref, g_ref, win_ref, cw_ref, cb_ref, lng_ref, lnb_ref, wout_ref, o_ref,
                 ubuf_ref, c_ref, v_ref, hn_ref, gate_ref, *, tm, width):
    i = pl.program_id(1)

    @pl.when(i == 0)
    def _():
        ubuf_ref[0:CONV_HALO, :] = jnp.zeros((CONV_HALO, width), F32)

    lng = lng_ref[...]
    lnb = lnb_ref[...]
    base = CONV_HALO - (CONV_K - 1)
    nsub = tm // CONV_SUB
    strips = [slice(c0, c0 + CONV_STRIP) for c0 in range(0, width, CONV_STRIP)]

    def norm_piece(s):
        rows = slice(CONV_SUB * s, CONV_SUB * (s + 1))
        hn_ref[rows, :] = _rms(h_ref[rows, :], g_ref[...]).astype(BF16)

    def proj_piece(s, cols):
        rows = slice(CONV_SUB * s, CONV_SUB * (s + 1))
        hn = hn_ref[rows, :]
        part = [jnp.dot(hn, win_ref[:, n * width + cols.start:n * width + cols.stop], preferred_element_type=F32)
                for n in range(3)]
        ubuf_ref[CONV_HALO + rows.start:CONV_HALO + rows.stop, cols] = part[0] * jax.nn.sigmoid(part[1])
        gate_ref[rows, cols] = part[2]
        return gate_ref[rows.start:rows.start + SUBLANES, cols.start:cols.start + LANES]

    def conv_piece(r0, cols, token=None):
        acc = jnp.broadcast_to(cb_ref[:, cols], (CONV_ROWS, CONV_STRIP))
        for r in range(SUBLANES):
            nrows = CONV_ROWS + (SUBLANES if r else 0)
            part = None
            for a in range((base + CONV_K - 1) // SUBLANES + 1):
                k = SUBLANES * a + r - base
                if 0 <= k < CONV_K:
                    row = r0 + SUBLANES * a
                    term = cw_ref[k:k + 1, cols] * ubuf_ref[row:row + nrows, cols]
                    part = term if part is None else part + term
            acc = acc + part[r:r + CONV_ROWS, :]
        c_ref[r0:r0 + CONV_ROWS, cols] = acc
        if token is not None:
            tile = (slice(r0, r0 + SUBLANES), slice(cols.start, cols.start + LANES))
            c_ref[tile] = jnp.where(i < 0, token, c_ref[tile])

    def post_piece(r0):
        rows = slice(r0, r0 + CONV_CHUNK)
        acc = c_ref[rows, :]
        mu = jnp.mean(acc, axis=-1, keepdims=True)
        cen = acc - mu
        var = jnp.mean(cen * cen, axis=-1, keepdims=True)
        y = cen * lax.rsqrt(var + LN_EPS) * lng + lnb
        v_ref[rows, :] = (_silu(y) * _silu(gate_ref[rows, :])).astype(BF16)

    def out_piece(s, cols):
        rows = slice(CONV_SUB * s, CONV_SUB * (s + 1))
        o_ref[rows, cols] = h_ref[rows, cols] + jnp.dot(v_ref[rows, :], wout_ref[:, cols],
                                                        preferred_element_type=F32)
        return o_ref[rows.start:rows.start + SUBLANES, cols.start:cols.start + LANES]

    norm_piece(0)
    for cols in strips:
        proj_piece(0, cols)
    for s in range(nsub):
        side = []
        if s + 1 < nsub:
            norm_piece(s + 1)
            side += [functools.partial(proj_piece, s + 1, cols) for cols in strips]
        if s >= 1:
            side += [functools.partial(out_piece, s - 1, cols) for cols in strips]
        main = [functools.partial(conv_piece, r0, cols)
                for r0 in range(CONV_SUB * s, CONV_SUB * (s + 1), CONV_ROWS) for cols in strips]
        every = max(1, (len(main) - 1) // max(1, len(side)))
        token = None
        for n, piece in enumerate(main):
            piece(token=token)
            token = side.pop(0)() if (n % every == every - 1 and side) else None
        for piece in side:
            piece()
        for r0 in range(CONV_SUB * s, CONV_SUB * (s + 1), CONV_CHUNK):
            post_piece(r0)
    for cols in strips:
        out_piece(nsub - 1, cols)

    ubuf_ref[0:CONV_HALO, :] = ubuf_ref[tm:tm + CONV_HALO, :]


def _row_spec(tm, d, nt, seq_major):
    if seq_major:
        return pl.BlockSpec((tm, d), lambda b, i: (i, b))
    return pl.BlockSpec((tm, d), lambda b, i: (b * nt + i, 0))


def _conv_layer(h2d, bsz, seq, seq_major_in, seq_major_out, g, w_in, conv_w, conv_b, ln_g, ln_b, w_out, tm):
    d = g.shape[-1]
    width = conv_w.shape[-1]
    nt = seq // tm
    out_shape = (seq, bsz * d) if seq_major_out else (bsz * seq, d)
    kern = functools.partial(_conv_kernel, tm=tm, width=width)
    return pl.pallas_call(
        kern,
        grid=(bsz, nt),
        in_specs=[
            _row_spec(tm, d, nt, seq_major_in),
            _const_spec((1, d)),
            _const_spec((d, 3 * width)),
            _const_spec((CONV_K, width)),
            _const_spec((1, width)),
            _const_spec((1, width)),
            _const_spec((1, width)),
            _const_spec((width, d)),
        ],
        out_specs=_row_spec(tm, d, nt, seq_major_out),
        out_shape=jax.ShapeDtypeStruct(out_shape, F32),
        scratch_shapes=[pltpu.VMEM((tm + CONV_HALO, width), F32), pltpu.VMEM((tm, width), F32),
                        pltpu.VMEM((tm, width), BF16), pltpu.VMEM((tm, d), BF16), pltpu.VMEM((tm, width), F32)],
        compiler_params=_params(("arbitrary", "arbitrary")),
        name="conv_layer",
    )(h2d, g.reshape(1, d), w_in.astype(BF16), conv_w, conv_b.reshape(1, width),
      ln_g.reshape(1, width), ln_b.reshape(1, width), w_out.astype(BF16))


def _split3(x):
    p1 = x.astype(BF16)
    r1 = x - p1.astype(F32)
    p2 = r1.astype(BF16)
    r2 = r1 - p2.astype(F32)
    return p1, p2, r2.astype(BF16)


def _fox_proj_kernel(h_ref, g_ref, wq_ref, wk_ref, wvt_ref, wg_ref, wf_ref, fb_ref, qg_ref, kg_ref,
                     seg_ref, tri_ref, place_ref, q_ref, k_ref, kc_ref, vt_ref, gate_ref, carry_ref,
                     *, tm, width):
    i = pl.program_id(1)
    heads = width // HEAD_DIM

    @pl.when(i == 0)
    def _():
        carry_ref[...] = jnp.zeros(carry_ref.shape, F32)

    hn = _rms(h_ref[...], g_ref[...]).astype(BF16)
    seg = seg_ref[...]

    def head_norm(x, gain):
        x2 = x * x
        hi = x2.astype(BF16)
        lo = (x2 - hi.astype(F32)).astype(BF16)
        parts = []
        for s in range(width // MXU_DIM):
            sl = slice(MXU_DIM * s, MXU_DIM * (s + 1))
            parts.append(jnp.dot(hi[:, sl], seg, preferred_element_type=F32)
                         + jnp.dot(lo[:, sl], seg, preferred_element_type=F32))
        ss = jnp.concatenate(parts, axis=-1)
        return x * lax.rsqrt(ss * (1.0 / HEAD_DIM) + RMS_EPS) * gain

    zq = jnp.dot(hn, wq_ref[...], preferred_element_type=F32)
    q_ref[...] = (head_norm(zq, qg_ref[...]) * (LOG2E / math.sqrt(HEAD_DIM))).astype(BF16)
    zk = jnp.dot(hn, wk_ref[...], preferred_element_type=F32)
    k_ref[...] = head_norm(zk, kg_ref[...]).astype(BF16)
    vt_ref[...] = lax.dot_general(wvt_ref[...], hn, NT_DIMS, preferred_element_type=F32).astype(BF16)
    gate_ref[...] = jnp.dot(hn, wg_ref[...], preferred_element_type=F32)

    zf = jnp.dot(hn, wf_ref[...], preferred_element_type=F32) + fb_ref[...]
    logf = jnp.minimum(zf, 0.0) - jnp.log1p(jnp.exp(-jnp.abs(zf)))
    tri = tri_ref[...]
    p1, p2, p3 = _split3(logf)
    local = (jnp.dot(tri, p1, preferred_element_type=F32)
             + jnp.dot(tri, p2, preferred_element_type=F32)
             + jnp.dot(tri, p3, preferred_element_type=F32))
    cum = local + carry_ref[0:1, :]
    carry_ref[...] = jnp.broadcast_to(cum[tm - 1:tm, :], carry_ref.shape)
    d1, d2, d3 = _split3(cum * (-LOG2E))
    lane = lax.broadcasted_iota(jnp.int32, (tm, LANES), 1)
    parts = jnp.where(lane < heads, d1, jnp.where(lane < 2 * heads, d2, d3))
    kc_ref[...] = jnp.dot(parts, place_ref[...], preferred_element_type=F32).astype(BF16)


def _fox_proj(h2d, bsz, seq, g, w_in, f_bias, q_g, k_g, tm):
    d = g.shape[-1]
    width = (w_in.shape[-1] // (4 * HEAD_DIM + 1)) * HEAD_DIM
    heads = width // HEAD_DIM
    assert DECAY_PARTS * heads <= LANES
    nt = seq // tm
    wb = w_in.astype(BF16)
    wq, wk, wv, wg = (wb[:, n * width:(n + 1) * width] for n in range(4))
    wf = jnp.pad(jnp.tile(wb[:, 4 * width:], (1, DECAY_PARTS)), ((0, 0), (0, LANES - DECAY_PARTS * heads)))
    fb = jnp.pad(jnp.tile(f_bias, DECAY_PARTS), (0, LANES - DECAY_PARTS * heads)).reshape(1, LANES)
    seg_id = jnp.arange(MXU_DIM) // HEAD_DIM
    seg = (seg_id[:, None] == seg_id[None, :]).astype(BF16)
    tri = (jnp.arange(tm)[:, None] >= jnp.arange(tm)[None, :]).astype(BF16)
    src = jnp.arange(LANES)
    part, head = src // heads, src % heads
    dst = (head // HEADS_PER_SLAB) * LANES + DECAY_PARTS * (head % HEADS_PER_SLAB) + part
    place = ((dst[:, None] == jnp.arange(width)[None, :]) & (src[:, None] < DECAY_PARTS * heads)).astype(BF16)
    row = lambda n: pl.BlockSpec((tm, n), lambda b, i: (b * nt + i, 0))
    kern = functools.partial(_fox_proj_kernel, tm=tm, width=width)
    m = bsz * seq
    return pl.pallas_call(
        kern,
        grid=(bsz, nt),
        in_specs=[row(d), _const_spec((1, d)), _const_spec((d, width)), _const_spec((d, width)),
                  _const_spec((width, d)), _const_spec((d, width)),
                  _const_spec((d, LANES)), _const_spec((1, LANES)), _const_spec((1, width)), _const_spec((1, width)),
                  _const_spec((MXU_DIM, MXU_DIM)), _const_spec((tm, tm)), _const_spec((LANES, width))],
        out_specs=[row(width), row(width), row(width),
                   pl.BlockSpec((None, None, width, tm), lambda b, i: (b, i, 0, 0)), row(width)],
        out_shape=[jax.ShapeDtypeStruct((m, width), BF16)] * 3
        + [jax.ShapeDtypeStruct((bsz, nt, width, tm), BF16), jax.ShapeDtypeStruct((m, width), F32)],
        scratch_shapes=[pltpu.VMEM((SUBLANES, LANES), F32)],
        compiler_params=_params(("arbitrary", "arbitrary")),
        name="fox_proj",
    )(h2d, g.reshape(1, d), wq, wk, wv.T, wg, wf, fb,
      jnp.tile(q_g, heads).reshape(1, width), jnp.tile(k_g, heads).reshape(1, width), seg, tri, place)


def _attn_kernel(q_ref, k_ref, kc_ref, vt_ref, gate_ref, o_ref, qt_ref, st_ref, acc_ref, mx_ref, *, tq, npairs):
    i = pl.program_id(2)
    nchains = HEADS_PER_SLAB * npairs
    lane = lax.broadcasted_iota(jnp.int32, (tq, LANES), 1)
    key_minus_query = (lax.broadcasted_iota(jnp.int32, (tq, tq), 0) - lax.broadcasted_iota(jnp.int32, (tq, tq), 1))

    def band(lo, hi):
        return jnp.where(lane < hi, 1.0, 0.0) * jnp.where(lane < lo, 0.0, 1.0)

    for p in range(npairs):
        q = q_ref[:, LANES * p:LANES * (p + 1)].astype(F32)
        for e in range(HEADS_PER_SLAB):
            q_aug = jnp.concatenate(
                [q * band(HEAD_DIM * e, HEAD_DIM * (e + 1)), band(DECAY_PARTS * e, DECAY_PARTS * (e + 1))], axis=-1)
            qt_ref[HEADS_PER_SLAB * p + e] = q_aug.T.astype(BF16)
    acc_ref[...] = jnp.zeros(acc_ref.shape, F32)

    mx_ref[...] = jnp.full(mx_ref.shape, -jnp.inf, F32)
    nblocks = i + 1
    ngroups = nblocks // ATTN_UNROLL

    def scores(j0, count):
        for u in range(count):
            j = j0 + u
            rows = pl.ds(pl.multiple_of(j * tq, tq), tq)
            visible = key_minus_query <= (i - j) * tq
            for p in range(npairs):
                slab = slice(LANES * p, LANES * (p + 1))
                k_aug = jnp.concatenate([k_ref[rows, slab], kc_ref[rows, slab]], axis=-1)
                for e in range(HEADS_PER_SLAB):
                    c = HEADS_PER_SLAB * p + e
                    st = jnp.dot(k_aug, qt_ref[c], preferred_element_type=F32)
                    st = jnp.where(visible, st, -jnp.inf)
                    st_ref[c, j] = st
                    mx_ref[c] = jnp.maximum(mx_ref[c], jnp.max(st.reshape(tq // SUBLANES, SUBLANES, tq), axis=0))

    def accumulate(j0, count, col_max):
        for p in range(npairs):
            vt = jnp.concatenate([vt_ref[j0 + u, LANES * p:LANES * (p + 1), :] for u in range(count)], axis=-1)
            v_aug = jnp.concatenate([vt, jnp.ones((ONES_ROWS, count * tq), BF16)], axis=0)
            for e in range(HEADS_PER_SLAB):
                c = HEADS_PER_SLAB * p + e
                pt = jnp.concatenate(
                    [jnp.exp2(st_ref[c, j0 + u] - col_max[c]).astype(BF16) for u in range(count)], axis=0)
                acc_ref[c] += jnp.dot(v_aug, pt, preferred_element_type=F32)

    def grouped(fn):
        def body(g, carry):
            fn(g * ATTN_UNROLL, ATTN_UNROLL)
            return carry
        lax.fori_loop(0, ngroups, body, 0)
        for rem in range(1, ATTN_UNROLL):
            @pl.when(nblocks - ngroups * ATTN_UNROLL == rem)
            def _(rem=rem):
                fn(ngroups * ATTN_UNROLL, rem)

    grouped(scores)
    col_max = [jnp.max(mx_ref[c], axis=0, keepdims=True) for c in range(nchains)]
    grouped(functools.partial(accumulate, col_max=col_max))

    for p in range(npairs):
        halves = []
        for e in range(HEADS_PER_SLAB):
            acc = acc_ref[HEADS_PER_SLAB * p + e]
            halves.append(acc[HEAD_DIM * e:HEAD_DIM * (e + 1), :] / acc[LANES:LANES + 1, :])
        o = jnp.concatenate(halves, axis=0).T
        slab = slice(LANES * p, LANES * (p + 1))
        o_ref[:, slab] = (o * _silu(gate_ref[:, slab])).astype(BF16)


def _fox_attn(q, k, kc, vt, gate, bsz, seq, tq):
    m, width = q.shape
    nq = seq // tq
    npairs = min(ATTN_PAIRS, width // LANES)
    nchains = HEADS_PER_SLAB * npairs
    cols = npairs * LANES
    qspec = pl.BlockSpec((tq, cols), lambda b, p, i: (b * nq + i, p))
    kspec = pl.BlockSpec((seq, cols), lambda b, p, i: (b, p))
    return pl.pallas_call(
        functools.partial(_attn_kernel, tq=tq, npairs=npairs),
        grid=(bsz, width // cols, nq),
        in_specs=[qspec, kspec, kspec,
                  pl.BlockSpec((None, nq, cols, tq), lambda b, p, i: (b, 0, p, 0)),
                  qspec],
        out_specs=qspec,
        out_shape=jax.ShapeDtypeStruct((m, width), BF16),
        scratch_shapes=[pltpu.VMEM((nchains, 2 * LANES, tq), BF16),
                        pltpu.VMEM((nchains, nq, tq, tq), F32),
                        pltpu.VMEM((nchains, LANES + ONES_ROWS, tq), F32),
                        pltpu.VMEM((nchains, SUBLANES, tq), F32)],
        compiler_params=_params(("arbitrary", "arbitrary", "arbitrary")),
        name="fox_attn",
    )(q, k, kc, vt, gate)


def _out_proj_kernel(x_ref, w_ref, h_ref, o_ref):
    o_ref[...] = h_ref[...] + jnp.dot(x_ref[...], w_ref[...], preferred_element_type=F32)


def _out_proj(x, w_out, h2d, bsz, seq, seq_major_out, tm):
    width, d = w_out.shape
    nt = seq // tm
    out_shape = (seq, bsz * d) if seq_major_out else (bsz * seq, d)
    return pl.pallas_call(
        _out_proj_kernel,
        grid=(bsz, nt),
        in_specs=[_row_spec(tm, width, nt, False), _const_spec((width, d)), _row_spec(tm, d, nt, False)],
        out_specs=_row_spec(tm, d, nt, seq_major_out),
        out_shape=jax.ShapeDtypeStruct(out_shape, F32),
        compiler_params=_params(("arbitrary", "arbitrary")),
        name="fox_out",
    )(x, w_out.astype(BF16), h2d)


SSM_CHUNK_GROUPS = LANES // SSM_STATE
SSM_CHUNK_COLS = 2 * LANES
SSM_CHUNK_CH = SSM_CHUNK_GROUPS * SSM_GROUP
SSM_SCAN_CHUNKS = 8


def _ssm_kernel(h_ref, g_ref, win_ref, bmat_ref, ar_ref, ai_ref, cmat_ref, d_ref, wglu_ref, bglu_ref,
                wout_ref, o_ref, bu_ref, state_ref, *, ts, nb, width):
    i = pl.program_id(0)
    rows = ts * nb
    nchunk = width // SSM_CHUNK_CH
    per_slab = LANES // SSM_CHUNK_CH

    @pl.when(i == 0)
    def _():
        state_ref[...] = jnp.zeros(state_ref.shape, F32)

    h = h_ref[...].reshape(rows, width)
    hn = _rms(h, g_ref[...]).astype(BF16)
    z = jnp.dot(hn, win_ref[...], preferred_element_type=F32)
    u = z[:, :width]
    ub = u.astype(BF16)

    for c in range(nchunk):
        slab = c // per_slab
        bu_ref[:, SSM_CHUNK_COLS * c:SSM_CHUNK_COLS * (c + 1)] = jnp.dot(
            ub[:, LANES * slab:LANES * (slab + 1)], bmat_ref[c], preferred_element_type=F32)

    for c0 in range(0, nchunk, SSM_SCAN_CHUNKS):
        chunks = range(c0, c0 + SSM_SCAN_CHUNKS)
        ar = [jnp.broadcast_to(ar_ref[:, LANES * c:LANES * (c + 1)], (nb, LANES)) for c in chunks]
        ai = [jnp.broadcast_to(ai_ref[:, LANES * c:LANES * (c + 1)], (nb, LANES)) for c in chunks]

        def step(t, carry, chunks=chunks, ar=ar, ai=ai):
            r = pl.ds(pl.multiple_of(t * nb, nb), nb)
            new = []
            for n, c in enumerate(chunks):
                xr, xi = carry[2 * n], carry[2 * n + 1]
                re_cols = slice(SSM_CHUNK_COLS * c, SSM_CHUNK_COLS * c + LANES)
                im_cols = slice(SSM_CHUNK_COLS * c + LANES, SSM_CHUNK_COLS * (c + 1))
                nr = ar[n] * xr - ai[n] * xi + bu_ref[r, re_cols]
                ni = ar[n] * xi + ai[n] * xr + bu_ref[r, im_cols]
                bu_ref[r, re_cols] = nr
                bu_ref[r, im_cols] = ni
                new += [nr, ni]
            return tuple(new)

        init = tuple(state_ref[:, LANES * k:LANES * (k + 1)] for k in range(2 * c0, 2 * (c0 + SSM_SCAN_CHUNKS)))
        fin = lax.fori_loop(0, ts, step, init)
        for n, k in enumerate(range(2 * c0, 2 * (c0 + SSM_SCAN_CHUNKS))):
            state_ref[:, LANES * k:LANES * (k + 1)] = fin[n]

    slabs = []
    for s in range(width // LANES):
        acc = None
        for c in range(per_slab * s, per_slab * (s + 1)):
            xc = bu_ref[:, SSM_CHUNK_COLS * c:SSM_CHUNK_COLS * (c + 1)].astype(BF16)
            part = jnp.dot(xc, cmat_ref[c], preferred_element_type=F32)
            acc = part if acc is None else acc + part
        slabs.append(acc)
    y = jnp.concatenate(slabs, axis=-1) + d_ref[...] * u
    gl = jax.nn.gelu(y)
    y = gl * jax.nn.sigmoid(jnp.dot(gl.astype(BF16), wglu_ref[...], preferred_element_type=F32) + bglu_ref[...])
    v = (y * _silu(z[:, width:])).astype(BF16)
    out = h + jnp.dot(v, wout_ref[...], preferred_element_type=F32)
    o_ref[...] = out.reshape(ts, nb, width)


def _ssm_discretize(log_dt, a_re, a_im, b_re, b_im, c_re, c_im):
    groups, state = a_re.shape
    nchunk = groups // SSM_CHUNK_GROUPS
    per_slab = LANES // SSM_CHUNK_CH
    dt = jnp.exp(log_dt)[:, None]
    mag = jnp.exp(a_re * dt)
    ang = a_im * dt
    abar_re = mag * jnp.cos(ang)
    abar_im = mag * jnp.sin(ang)
    den = a_re * a_re + a_im * a_im
    nr = abar_re - 1.0
    ni = abar_im
    zr = ((nr * a_re + ni * a_im) / den)[..., None]
    zi = ((ni * a_re - nr * a_im) / den)[..., None]
    bb_re = zr * b_re - zi * b_im
    bb_im = zr * b_im + zi * b_re
    eye = jnp.eye(SSM_CHUNK_GROUPS, dtype=F32)
    slot = jax.nn.one_hot(jnp.arange(nchunk) % per_slab, per_slab, dtype=F32)

    def b_block(bb):
        x = bb.transpose(0, 2, 1).reshape(nchunk, SSM_CHUNK_GROUPS, SSM_GROUP, state)
        return jnp.einsum("cgip,gh->cgihp", x, eye).reshape(nchunk, SSM_CHUNK_CH, LANES)

    bblk = jnp.concatenate([b_block(bb_re), b_block(bb_im)], axis=-1)
    bmat = jnp.einsum("cq,crn->cqrn", slot, bblk).reshape(nchunk, LANES, SSM_CHUNK_COLS)

    def c_block(cc):
        x = cc.transpose(0, 2, 1).reshape(nchunk, SSM_CHUNK_GROUPS, state, SSM_GROUP)
        return jnp.einsum("cgpo,gh->cgpho", x, eye).reshape(nchunk, LANES, SSM_CHUNK_CH)

    cblk = jnp.concatenate([c_block(c_re), -c_block(c_im)], axis=1)
    cmat = jnp.einsum("cq,ckr->ckqr", slot, cblk).reshape(nchunk, SSM_CHUNK_COLS, LANES)
    return (bmat.astype(BF16), abar_re.reshape(1, groups * state), abar_im.reshape(1, groups * state),
            cmat.astype(BF16))


def _ssm_layer(h3d, g, w_in, log_dt, a_re, a_im, b_re, b_im, c_re, c_im, d_skip, w_glu, b_glu, w_out, ts):
    seq, nb, d = h3d.shape
    width = d_skip.shape[-1]
    groups, state = a_re.shape
    assert nb == SUBLANES and state == SSM_STATE and width == groups * SSM_GROUP
    nchunk = groups // SSM_CHUNK_GROUPS
    assert nchunk % SSM_SCAN_CHUNKS == 0
    bmat, ar, ai, cmat = _ssm_discretize(log_dt, a_re, a_im, b_re, b_im, c_re, c_im)
    ncols = groups * state
    blk = pl.BlockSpec((ts, nb, d), lambda i: (i, 0, 0))
    kern = functools.partial(_ssm_kernel, ts=ts, nb=nb, width=width)
    return pl.pallas_call(
        kern,
        grid=(seq // ts,),
        in_specs=[blk, _const_spec((1, d)), _const_spec((d, 2 * width)),
                  _const_spec((nchunk, LANES, SSM_CHUNK_COLS)), _const_spec((1, ncols)), _const_spec((1, ncols)),
                  _const_spec((nchunk, SSM_CHUNK_COLS, LANES)), _const_spec((1, width)),
                  _const_spec((width, width)), _const_spec((1, width)), _const_spec((width, d))],
        out_specs=blk,
        out_shape=jax.ShapeDtypeStruct((seq, nb, d), F32),
        scratch_shapes=[pltpu.VMEM((ts * nb, 2 * ncols), F32), pltpu.VMEM((nb, 2 * ncols), F32)],
        compiler_params=_params(("arbitrary",)),
        name="ssm_layer",
    )(h3d, g.reshape(1, d), w_in.astype(BF16), bmat, ar, ai, cmat, d_skip.reshape(1, width),
      w_glu.astype(BF16), b_glu.reshape(1, width), w_out.astype(BF16))


def kernel(x, norm_g, a_w_in, a_conv_w, a_conv_b, a_ln_g, a_ln_b, a_w_out, b_w_in, b_f_bias, b_q_norm,
           b_k_norm, b_w_out, c_w_in, c_log_dt, c_a_re, c_a_im, c_b_re, c_b_im, c_c_re, c_c_im, c_d,
           c_w_glu, c_b_glu, c_w_out):
    bsz, seq, d = x.shape
    depth = norm_g.shape[0]
    tm = min(ROW_TILE, seq)
    tq = min(ATTN_TILE, seq)
    ts = min(SSM_STEPS, seq)
    h = x.reshape(bsz * seq, d)
    seq_major = False
    for layer in range(depth):
        kind, j = layer % 3, layer // 3
        next_major = (layer + 1 < depth) and ((layer + 1) % 3 == 2)
        if kind == 0:
            h = _conv_layer(h, bsz, seq, seq_major, next_major, norm_g[layer], a_w_in[j], a_conv_w[j],
                            a_conv_b[j], a_ln_g[j], a_ln_b[j], a_w_out[j], tm)
        elif kind == 1:
            assert not seq_major
            q, k, kc, vt, gate = _fox_proj(h, bsz, seq, norm_g[layer], b_w_in[j], b_f_bias[j],
                                           b_q_norm[j], b_k_norm[j], tq)
            og = _fox_attn(q, k, kc, vt, gate, bsz, seq, tq)
            h = _out_proj(og, b_w_out[j], h, bsz, seq, next_major, tm)
        else:
            assert seq_major
            h3 = _ssm_layer(h.reshape(seq, bsz, d), norm_g[layer], c_w_in[j], c_log_dt[j], c_a_re[j],
                            c_a_im[j], c_b_re[j], c_b_im[j], c_c_re[j], c_c_im[j], c_d[j], c_w_glu[j],
                            c_b_glu[j], c_w_out[j], ts)
            h = h3.reshape(seq, bsz * d)
            next_major = True
        seq_major = next_major
    if seq_major:
        h = h.reshape(seq, bsz, d).transpose(1, 0, 2)
    return h.reshape(bsz, seq, d)
```

```python
import functools
import math

import jax
import jax.numpy as jnp
from jax import lax
from jax.experimental import pallas as pl
from jax.experimental.pallas import tpu as pltpu

F32 = jnp.float32
BF16 = jnp.bfloat16

RMS_EPS = 1e-6
LN_EPS = 1e-5
CONV_K = 31
HEAD_DIM = 64
SSM_GROUP = 16
SSM_STATE = 64

LANES = 128
SUBLANES = 8
MXU_DIM = 256
VMEM_LIMIT_BYTES = 56 * 1024 * 1024

ROW_TILE = 512
ATTN_TILE = 256
SSM_STEPS = 64

ATTN_PAIRS = 4
ATTN_UNROLL = 4
HEADS_PER_SLAB = LANES // HEAD_DIM
DECAY_PARTS = 3
ONES_ROWS = 16
LOG2E = math.log2(math.e)
NT_DIMS = (((1,), (1,)), ((), ()))

CONV_HALO = 32
CONV_CHUNK = 32
CONV_ROWS = 64
CONV_STRIP = 256
CONV_SUB = 256


def _rms(x, g):
    ms = jnp.mean(x * x, axis=-1, keepdims=True)
    return x * lax.rsqrt(ms + RMS_EPS) * g


def _silu(x):
    return x * jax.nn.sigmoid(x)


def _const_spec(shape):
    zeros = (0,) * len(shape)
    return pl.BlockSpec(shape, lambda *_: zeros, pipeline_mode=pl.Buffered(1))


def _params(sem):
    return pltpu.CompilerParams(dimension_semantics=sem, vmem_limit_bytes=VMEM_LIMIT_BYTES)


def _conv_kernel(h_ref, g_ref, win_ref, cw_ref, cb_ref, lng_ref, lnb_ref, wout_ref, o_ref,
                 ubuf_ref, c_ref, v_ref, hn_ref, gate_ref, *, tm, width):
    i = pl.program_id(1)

    @pl.when(i == 0)
    def _():
        ubuf_ref[0:CONV_HALO, :] = jnp.zeros((CONV_HALO, width), F32)

    lng = lng_ref[...]
    lnb = lnb_ref[...]
    base = CONV_HALO - (CONV_K - 1)
    nsub = tm // CONV_SUB
    strips = [slice(c0, c0 + CONV_STRIP) for c0 in range(0, width, CONV_STRIP)]

    def norm_piece(s):
        rows = slice(CONV_SUB * s, CONV_SUB * (s + 1))
        hn_ref[rows, :] = _rms(h_ref[rows, :], g_ref[...]).astype(BF16)

    def proj_piece(s, cols):
        rows = slice(CONV_SUB * s, CONV_SUB * (s + 1))
        hn = hn_ref[rows, :]
        part = [jnp.dot(hn, win_ref[:, n * width + cols.start:n * width + cols.stop], preferred_element_type=F32)
                for n in range(3)]
        ubuf_ref[CONV_HALO + rows.start:CONV_HALO + rows.stop, cols] = part[0] * jax.nn.sigmoid(part[1])
        gate_ref[rows, cols] = part[2]
        return gate_ref[rows.start:rows.start + SUBLANES, cols.start:cols.start + LANES]

    def conv_piece(r0, cols, token=None):
        acc = jnp.broadcast_to(cb_ref[:, cols], (CONV_ROWS, CONV_STRIP))
        for r in range(SUBLANES):
            nrows = CONV_ROWS + (SUBLANES if r else 0)
            part = None
            for a in range((base + CONV_K - 1) // SUBLANES + 1):
                k = SUBLANES * a + r - base
                if 0 <= k < CONV_K:
                    row = r0 + SUBLANES * a
                    term = cw_ref[k:k + 1, cols] * ubuf_ref[row:row + nrows, cols]
                    part = term if part is None else part + term
            acc = acc + part[r:r + CONV_ROWS, :]
        c_ref[r0:r0 + CONV_ROWS, cols] = acc
        if token is not None:
            tile = (slice(r0, r0 + SUBLANES), slice(cols.start, cols.start + LANES))
            c_ref[tile] = jnp.where(i < 0, token, c_ref[tile])

    def post_piece(r0):
        rows = slice(r0, r0 + CONV_CHUNK)
        acc = c_ref[rows, :]
        mu = jnp.mean(acc, axis=-1, keepdims=True)
        cen = acc - mu
        var = jnp.mean(cen * cen, axis=-1, keepdims=True)
        y = cen * lax.rsqrt(var + LN_EPS) * lng + lnb
        v_ref[rows, :] = (_silu(y) * _silu(gate_ref[rows, :])).astype(BF16)

    def out_piece(s, cols):
        rows = slice(CONV_SUB * s, CONV_SUB * (s + 1))
        o_ref[rows, cols] = h_ref[rows, cols] + jnp.dot(v_ref[rows, :], wout_ref[:, cols],
                                                        preferred_element_type=F32)
        return o_ref[rows.start:rows.start + SUBLANES, cols.start:cols.start + LANES]

    norm_piece(0)
    for cols in strips:
        proj_piece(0, cols)
    for s in range(nsub):
        side = []
        if s + 1 < nsub:
            norm_piece(s + 1)
            side += [functools.partial(proj_piece, s + 1, cols) for cols in strips]
        if s >= 1:
            side += [functools.partial(out_piece, s - 1, cols) for cols in strips]
        main = [functools.partial(conv_piece, r0, cols)
                for r0 in range(CONV_SUB * s, CONV_SUB * (s + 1), CONV_ROWS) for cols in strips]
        every = max(1, (len(main) - 1) // max(1, len(side)))
        token = None
        for n, piece in enumerate(main):
            piece(token=token)
            token = side.pop(0)() if (n % every == every - 1 and side) else None
        for piece in side:
            piece()
        for r0 in range(CONV_SUB * s, CONV_SUB * (s + 1), CONV_CHUNK):
            post_piece(r0)
    for cols in strips:
        out_piece(nsub - 1, cols)

    ubuf_ref[0:CONV_HALO, :] = ubuf_ref[tm:tm + CONV_HALO, :]


def _row_spec(tm, d, nt, seq_major):
    if seq_major:
        return pl.BlockSpec((tm, d), lambda b, i: (i, b))
    return pl.BlockSpec((tm, d), lambda b, i: (b * nt + i, 0))


def _conv_layer(h2d, bsz, seq, seq_major_in, seq_major_out, g, w_in, conv_w, conv_b, ln_g, ln_b, w_out, tm):
    d = g.shape[-1]
    width = conv_w.shape[-1]
    nt = seq // tm
    out_shape = (seq, bsz * d) if seq_major_out else (bsz * seq, d)
    kern = functools.partial(_conv_kernel, tm=tm, width=width)
    return pl.pallas_call(
        kern,
        grid=(bsz, nt),
        in_specs=[
            _row_spec(tm, d, nt, seq_major_in),
            _const_spec((1, d)),
            _const_spec((d, 3 * width)),
            _const_spec((CONV_K, width)),
            _const_spec((1, width)),
            _const_spec((1, width)),
            _const_spec((1, width)),
            _const_spec((width, d)),
        ],
        out_specs=_row_spec(tm, d, nt, seq_major_out),
        out_shape=jax.ShapeDtypeStruct(out_shape, F32),
        scratch_shapes=[pltpu.VMEM((tm + CONV_HALO, width), F32), pltpu.VMEM((tm, width), F32),
                        pltpu.VMEM((tm, width), BF16), pltpu.VMEM((tm, d), BF16), pltpu.VMEM((tm, width), F32)],
        compiler_params=_params(("arbitrary", "arbitrary")),
        name="conv_layer",
    )(h2d, g.reshape(1, d), w_in.astype(BF16), conv_w, conv_b.reshape(1, width),
      ln_g.reshape(1, width), ln_b.reshape(1, width), w_out.astype(BF16))


def _split3(x):
    p1 = x.astype(BF16)
    r1 = x - p1.astype(F32)
    p2 = r1.astype(BF16)
    r2 = r1 - p2.astype(F32)
    return p1, p2, r2.astype(BF16)


def _fox_proj_kernel(h_ref, g_ref, wq_ref, wk_ref, wvt_ref, wg_ref, wf_ref, fb_ref, qg_ref, kg_ref,
                     seg_ref, tri_ref, place_ref, q_ref, k_ref, kc_ref, vt_ref, gate_ref, carry_ref,
                     *, tm, width):
    i = pl.program_id(1)
    heads = width // HEAD_DIM

    @pl.when(i == 0)
    def _():
        carry_ref[...] = jnp.zeros(carry_ref.shape, F32)

    hn = _rms(h_ref[...], g_ref[...]).astype(BF16)
    seg = seg_ref[...]

    def head_norm(x, gain):
        x2 = x * x
        hi = x2.astype(BF16)
        lo = (x2 - hi.astype(F32)).astype(BF16)
        parts = []
        for s in range(width // MXU_DIM):
            sl = slice(MXU_DIM * s, MXU_DIM * (s + 1))
            parts.append(jnp.dot(hi[:, sl], seg, preferred_element_type=F32)
                         + jnp.dot(lo[:, sl], seg, preferred_element_type=F32))
        ss = jnp.concatenate(parts, axis=-1)
        return x * lax.rsqrt(ss * (1.0 / HEAD_DIM) + RMS_EPS) * gain

    zq = jnp.dot(hn, wq_ref[...], preferred_element_type=F32)
    q_ref[...] = (head_norm(zq, qg_ref[...]) * (LOG2E / math.sqrt(HEAD_DIM))).astype(BF16)
    zk = jnp.dot(hn, wk_ref[...], preferred_element_type=F32)
    k_ref[...] = head_norm(zk, kg_ref[...]).astype(BF16)
    vt_ref[...] = lax.dot_general(wvt_ref[...], hn, NT_DIMS, preferred_element_type=F32).astype(BF16)
    gate_ref[...] = jnp.dot(hn, wg_ref[...], preferred_element_type=F32)

    zf = jnp.dot(hn, wf_ref[...], preferred_element_type=F32) + fb_ref[...]
    logf = jnp.minimum(zf, 0.0) - jnp.log1p(jnp.exp(-jnp.abs(zf)))
    tri = tri_ref[...]
    p1, p2, p3 = _split3(logf)
    local = (jnp.dot(tri, p1, preferred_element_type=F32)
             + jnp.dot(tri, p2, preferred_element_type=F32)
             + jnp.dot(tri, p3, preferred_element_type=F32))
    cum = local + carry_ref[0:1, :]
    carry_ref[...] = jnp.broadcast_to(cum[tm - 1:tm, :], carry_ref.shape)
    d1, d2, d3 = _split3(cum * (-LOG2E))
    lane = lax.broadcasted_iota(jnp.int32, (tm, LANES), 1)
    parts = jnp.where(lane < heads, d1, jnp.where(lane < 2 * heads, d2, d3))
    kc_ref[...] = jnp.dot(parts, place_ref[...], preferred_element_type=F32).astype(BF16)


def _fox_proj(h2d, bsz, seq, g, w_in, f_bias, q_g, k_g, tm):
    d = g.shape[-1]
    width = (w_in.shape[-1] // (4 * HEAD_DIM + 1)) * HEAD_DIM
    heads = width // HEAD_DIM
    assert DECAY_PARTS * heads <= LANES
    nt = seq // tm
    wb = w_in.astype(BF16)
    wq, wk, wv, wg = (wb[:, n * width:(n + 1) * width] for n in range(4))
    wf = jnp.pad(jnp.tile(wb[:, 4 * width:], (1, DECAY_PARTS)), ((0, 0), (0, LANES - DECAY_PARTS * heads)))
    fb = jnp.pad(jnp.tile(f_bias, DECAY_PARTS), (0, LANES - DECAY_PARTS * heads)).reshape(1, LANES)
    seg_id = jnp.arange(MXU_DIM) // HEAD_DIM
    seg = (seg_id[:, None] == seg_id[None, :]).astype(BF16)
    tri = (jnp.arange(tm)[:, None] >= jnp.arange(tm)[None, :]).astype(BF16)
    src = jnp.arange(LANES)
    part, head = src // heads, src % heads
    dst = (head // HEADS_PER_SLAB) * LANES + DECAY_PARTS * (head % HEADS_PER_SLAB) + part
    place = ((dst[:, None] == jnp.arange(width)[None, :]) & (src[:, None] < DECAY_PARTS * heads)).astype(BF16)
    row = lambda n: pl.BlockSpec((tm, n), lambda b, i: (b * nt + i, 0))
    kern = functools.partial(_fox_proj_kernel, tm=tm, width=width)
    m = bsz * seq
    return pl.pallas_call(
        kern,
        grid=(bsz, nt),
        in_specs=[row(d), _const_spec((1, d)), _const_spec((d, width)), _const_spec((d, width)),
                  _const_spec((width, d)), _const_spec((d, width)),
                  _const_spec((d, LANES)), _const_spec((1, LANES)), _const_spec((1, width)), _const_spec((1, width)),
                  _const_spec((MXU_DIM, MXU_DIM)), _const_spec((tm, tm)), _const_spec((LANES, width))],
        out_specs=[row(width), row(width), row(width),
                   pl.BlockSpec((None, None, width, tm), lambda b, i: (b, i, 0, 0)), row(width)],
        out_shape=[jax.ShapeDtypeStruct((m, width), BF16)] * 3
        + [jax.ShapeDtypeStruct((bsz, nt, width, tm), BF16), jax.ShapeDtypeStruct((m, width), F32)],
        scratch_shapes=[pltpu.VMEM((SUBLANES, LANES), F32)],
        compiler_params=_params(("arbitrary", "arbitrary")),
        name="fox_proj",
    )(h2d, g.reshape(1, d), wq, wk, wv.T, wg, wf, fb,
      jnp.tile(q_g, heads).reshape(1, width), jnp.tile(k_g, heads).reshape(1, width), seg, tri, place)


def _attn_kernel(q_ref, k_ref, kc_ref, vt_ref, gate_a_ref, gate_b_ref, oa_ref, ob_ref,
                 qt_ref, sta_ref, stb_ref, acc_ref, mx_ref, cm_ref, *, tq, npairs, nq):
    i = pl.program_id(2)
    half = npairs // 2
    nch = HEADS_PER_SLAB * half
    st_refs = (sta_ref, stb_ref)
    lane = lax.broadcasted_iota(jnp.int32, (tq, LANES), 1)
    on_or_below_diagonal = (lax.broadcasted_iota(jnp.int32, (tq, tq), 0)
                            <= lax.broadcasted_iota(jnp.int32, (tq, tq), 1))

    def band(lo, hi):
        return jnp.where(lane < hi, 1.0, 0.0) * jnp.where(lane < lo, 0.0, 1.0)

    ones = jnp.ones((ONES_ROWS, tq), BF16)

    def score_piece(hf, n, j, diagonal):
        c = nch * hf + n
        slab = slice(LANES * (c // HEADS_PER_SLAB), LANES * (c // HEADS_PER_SLAB + 1))
        rows = pl.ds(pl.multiple_of(j * tq, tq), tq)
        k_aug = jnp.concatenate([k_ref[rows, slab], kc_ref[rows, slab]], axis=-1)
        st = jnp.dot(k_aug, qt_ref[c], preferred_element_type=F32)
        if diagonal:
            st = jnp.where(on_or_below_diagonal, st, -jnp.inf)
        st_refs[hf][n, j] = st
        mx_ref[c] = jnp.maximum(mx_ref[c], jnp.max(st.reshape(tq // SUBLANES, SUBLANES, tq), axis=0))

    def acc_piece(hf, n, j):
        c = nch * hf + n
        v0 = HEAD_DIM * c
        v_aug = jnp.concatenate([vt_ref[j, v0:v0 + HEAD_DIM, :], ones], axis=0)
        pt = jnp.exp2(st_refs[hf][n, j] - cm_ref[c]).astype(BF16)
        acc_ref[c] += jnp.dot(v_aug, pt, preferred_element_type=F32)

    def scores(hf, j0, count, diagonal):
        for u in range(count):
            for n in range(nch):
                score_piece(hf, n, j0 + u, diagonal)

    def accumulate(hf, j0, count):
        for u in range(count):
            for n in range(nch):
                acc_piece(hf, n, j0 + u)

    def both(score_half, j0, count, diagonal):
        for u in range(count):
            for n in range(nch):
                score_piece(score_half, n, j0 + u, diagonal)
                acc_piece(1 - score_half, n, j0 + u)

    def grouped(fn, nblocks):
        ngroups = nblocks // ATTN_UNROLL

        def body(g, carry):
            fn(g * ATTN_UNROLL, ATTN_UNROLL)
            return carry
        lax.fori_loop(0, ngroups, body, 0)
        for rem in range(1, ATTN_UNROLL):
            @pl.when(nblocks - ngroups * ATTN_UNROLL == rem)
            def _(rem=rem):
                fn(ngroups * ATTN_UNROLL, rem)

    def reset(hf, ref, value):
        ref[nch * hf:nch * (hf + 1)] = jnp.full((nch,) + ref.shape[1:], value, F32)

    def column_max(hf):
        for c in range(nch * hf, nch * (hf + 1)):
            cm_ref[c] = jnp.max(mx_ref[c], axis=0, keepdims=True)

    def finish(hf, gate_ref, o_ref):
        for p in range(half):
            halves = []
            for e in range(HEADS_PER_SLAB):
                acc = acc_ref[nch * hf + HEADS_PER_SLAB * p + e]
                halves.append(acc[:HEAD_DIM, :] / acc[HEAD_DIM:HEAD_DIM + 1, :])
            o = jnp.concatenate(halves, axis=0).T
            slab = slice(LANES * p, LANES * (p + 1))
            o_ref[:, slab] = (o * _silu(gate_ref[:, slab])).astype(BF16)

    reset(1, acc_ref, 0.0)

    @pl.when(i < nq)
    def _():
        for p in range(npairs):
            q = q_ref[:, LANES * p:LANES * (p + 1)].astype(F32)
            for e in range(HEADS_PER_SLAB):
                q_aug = jnp.concatenate(
                    [q * band(HEAD_DIM * e, HEAD_DIM * (e + 1)), band(DECAY_PARTS * e, DECAY_PARTS * (e + 1))],
                    axis=-1)
                qt_ref[HEADS_PER_SLAB * p + e] = q_aug.T.astype(BF16)
        reset(0, mx_ref, -jnp.inf)

        grouped(lambda j0, count: both(0, j0, count, False), i)
        scores(0, i, 1, True)

    @pl.when(i == nq)
    def _():
        grouped(functools.partial(accumulate, 1), nq)

    @pl.when(i >= 1)
    def _():
        finish(1, gate_b_ref, ob_ref)

    @pl.when(i < nq)
    def _():
        column_max(0)
        reset(0, acc_ref, 0.0)
        reset(1, mx_ref, -jnp.inf)

        grouped(lambda j0, count: both(1, j0, count, False), i)
        both(1, i, 1, True)
        finish(0, gate_a_ref, oa_ref)
        column_max(1)


def _fox_attn(q, k, kc, vt, gate, bsz, seq, tq):
    m, width = q.shape
    nq = seq // tq
    npairs = min(ATTN_PAIRS, width // LANES)
    assert npairs % 2 == 0
    nchains = HEADS_PER_SLAB * npairs
    cols = npairs * LANES
    hcols = cols // 2
    cur = lambda i: jnp.minimum(i, nq - 1)
    prev = lambda i: jnp.maximum(i - 1, 0)
    kspec = pl.BlockSpec((seq, cols), lambda b, p, i: (b, p))
    oa, ob = pl.pallas_call(
        functools.partial(_attn_kernel, tq=tq, npairs=npairs, nq=nq),
        grid=(bsz, width // cols, nq + 1),
        in_specs=[pl.BlockSpec((tq, cols), lambda b, p, i: (b * nq + cur(i), p)), kspec, kspec,
                  pl.BlockSpec((None, nq, cols, tq), lambda b, p, i: (b, 0, p, 0)),
                  pl.BlockSpec((tq, hcols), lambda b, p, i: (b * nq + cur(i), 2 * p)),
                  pl.BlockSpec((tq, hcols), lambda b, p, i: (b * nq + prev(i), 2 * p + 1))],
        out_specs=[pl.BlockSpec((tq, hcols), lambda b, p, i: (b * nq + cur(i), p)),
                   pl.BlockSpec((tq, hcols), lambda b, p, i: (b * nq + prev(i), p))],
        out_shape=[jax.ShapeDtypeStruct((m, width // 2), BF16)] * 2,
        scratch_shapes=[pltpu.VMEM((nchains, 2 * LANES, tq), BF16),
                        pltpu.VMEM((nchains // 2, nq, tq, tq), F32),
                        pltpu.VMEM((nchains // 2, nq, tq, tq), F32),
                        pltpu.VMEM((nchains, HEAD_DIM + ONES_ROWS, tq), F32),
                        pltpu.VMEM((nchains, SUBLANES, tq), F32),
                        pltpu.VMEM((nchains, 1, tq), F32)],
        compiler_params=_params(("arbitrary", "arbitrary", "arbitrary")),
        name="fox_attn",
    )(q, k, kc, vt, gate, gate)
    return oa, ob, width // cols


def _out_proj_kernel(xa_ref, xb_ref, wa_ref, wb_ref, h_ref, o_ref):
    o_ref[...] = (h_ref[...] + jnp.dot(xa_ref[...], wa_ref[...], preferred_element_type=F32)
                  + jnp.dot(xb_ref[...], wb_ref[...], preferred_element_type=F32))


def _out_proj(xa, xb, ngroups, w_out, h2d, bsz, seq, seq_major_out, tm):
    width, d = w_out.shape
    nt = seq // tm
    w = w_out.astype(BF16).reshape(ngroups, 2, width // (2 * ngroups), d)
    wa, wb = (w[:, n].reshape(width // 2, d) for n in range(2))
    out_shape = (seq, bsz * d) if seq_major_out else (bsz * seq, d)
    xspec = _row_spec(tm, width // 2, nt, False)
    return pl.pallas_call(
        _out_proj_kernel,
        grid=(bsz, nt),
        in_specs=[xspec, xspec, _const_spec((width // 2, d)), _const_spec((width // 2, d)),
                  _row_spec(tm, d, nt, False)],
        out_specs=_row_spec(tm, d, nt, seq_major_out),
        out_shape=jax.ShapeDtypeStruct(out_shape, F32),
        compiler_params=_params(("arbitrary", "arbitrary")),
        name="fox_out",
    )(xa, xb, wa, wb, h2d)


SSM_CHUNK_GROUPS = LANES // SSM_STATE
SSM_CHUNK_COLS = 2 * LANES
SSM_CHUNK_CH = SSM_CHUNK_GROUPS * SSM_GROUP
SSM_SCAN_CHUNKS = 8
SSM_SUB = 2


def _ssm_kernel(h_ref, g_ref, win_ref, bmat_ref, ar_ref, ai_ref, cmat_ref, d_ref, wglu_ref, bglu_ref,
                wout_ref, o_ref, bu_ref, state_ref, coef_ref, hn_ref, u_ref, ub_ref, gate_ref, gl_ref, v_ref,
                *, ts, nb, width):
    i = pl.program_id(0)
    rows = ts * nb
    rs, tsub = rows // SSM_SUB, ts // SSM_SUB
    nchunk = width // SSM_CHUNK_CH
    per_slab = LANES // SSM_CHUNK_CH
    ngroup = nchunk // SSM_SCAN_CHUNKS
    strips = [slice(c0, c0 + MXU_DIM) for c0 in range(0, width, MXU_DIM)]

    @pl.when(i == 0)
    def _():
        state_ref[...] = jnp.zeros(state_ref.shape, F32)
        coef_ref[0] = jnp.broadcast_to(ar_ref[...], coef_ref.shape[1:])
        coef_ref[1] = jnp.broadcast_to(ai_ref[...], coef_ref.shape[1:])

    def sub_rows(s):
        return slice(rs * s, rs * (s + 1))

    def norm_piece(s):
        h = h_ref[tsub * s:tsub * (s + 1)].reshape(rs, width)
        hn_ref[sub_rows(s), :] = _rms(h, g_ref[...]).astype(BF16)

    def in_piece(s, cols):
        r = sub_rows(s)
        hn = hn_ref[r, :]
        u = jnp.dot(hn, win_ref[:, cols], preferred_element_type=F32)
        u_ref[r, cols] = u
        ub_ref[r, cols] = u.astype(BF16)
        gate_ref[r, cols] = jnp.dot(hn, win_ref[:, width + cols.start:width + cols.stop],
                                    preferred_element_type=F32)
        return gate_ref[r.start:r.start + nb, cols.start:cols.start + LANES]

    def bproj_piece(s, c):
        r = sub_rows(s)
        slab = c // per_slab
        ccols = slice(SSM_CHUNK_COLS * c, SSM_CHUNK_COLS * (c + 1))
        bu_ref[r, ccols] = jnp.dot(ub_ref[r, LANES * slab:LANES * (slab + 1)], bmat_ref[c],
                                   preferred_element_type=F32)
        return bu_ref[r.start:r.start + nb, ccols.start:ccols.start + LANES]

    def scan_piece(s, t, g, carry, token):
        r = slice(rs * s + nb * t, rs * s + nb * (t + 1))
        new = []
        for n, c in enumerate(range(SSM_SCAN_CHUNKS * g, SSM_SCAN_CHUNKS * (g + 1))):
            xr, xi = carry[2 * n], carry[2 * n + 1]
            lanes = slice(LANES * c, LANES * (c + 1))
            ar, ai = coef_ref[0, :, lanes], coef_ref[1, :, lanes]
            re_cols = slice(SSM_CHUNK_COLS * c, SSM_CHUNK_COLS * c + LANES)
            im_cols = slice(SSM_CHUNK_COLS * c + LANES, SSM_CHUNK_COLS * (c + 1))
            nr = ar * xr - ai * xi + bu_ref[r, re_cols]
            ni = ar * xi + ai * xr + bu_ref[r, im_cols]
            if n == 0 and token is not None:
                nr = jnp.where(i < 0, token, nr)
            bu_ref[r, re_cols] = nr
            bu_ref[r, im_cols] = ni
            new += [nr, ni]
        return new

    def cproj_piece(s, slab):
        r = sub_rows(s)
        cols = slice(LANES * slab, LANES * (slab + 1))
        acc = None
        for c in range(per_slab * slab, per_slab * (slab + 1)):
            xc = bu_ref[r, SSM_CHUNK_COLS * c:SSM_CHUNK_COLS * (c + 1)].astype(BF16)
            part = jnp.dot(xc, cmat_ref[c], preferred_element_type=F32)
            acc = part if acc is None else acc + part
        gl = jax.nn.gelu(acc + d_ref[:, cols] * u_ref[r, cols])
        u_ref[r, cols] = gl
        gl_ref[r, cols] = gl.astype(BF16)
        return u_ref[r.start:r.start + nb, cols]

    def glu_piece(s, cols):
        r = sub_rows(s)
        t2 = jnp.dot(gl_ref[r, :], wglu_ref[:, cols], preferred_element_type=F32) + bglu_ref[:, cols]
        y = u_ref[r, cols] * jax.nn.sigmoid(t2)
        v_ref[r, cols] = (y * _silu(gate_ref[r, cols])).astype(BF16)
        return None

    def out_piece(s, cols):
        r = sub_rows(s)
        t = slice(tsub * s, tsub * (s + 1))
        h = h_ref[t, :, cols].reshape(rs, cols.stop - cols.start)
        out = h + jnp.dot(v_ref[r, :], wout_ref[:, cols], preferred_element_type=F32)
        o_ref[t, :, cols] = out.reshape(tsub, nb, cols.stop - cols.start)
        return o_ref[t.start, :, cols.start:cols.start + LANES]

    def before(s):
        return ([functools.partial(in_piece, s, cols) for cols in strips]
                + [functools.partial(bproj_piece, s, c) for c in range(nchunk)])

    def after(s):
        return ([functools.partial(cproj_piece, s, slab) for slab in range(width // LANES)]
                + [functools.partial(glu_piece, s, cols) for cols in strips]
                + [functools.partial(out_piece, s, cols) for cols in strips])

    norm_piece(0)
    for piece in before(0):
        piece()
    carry = [[state_ref[:, LANES * k:LANES * (k + 1)]
              for k in range(2 * SSM_SCAN_CHUNKS * g, 2 * SSM_SCAN_CHUNKS * (g + 1))] for g in range(ngroup)]
    for s in range(SSM_SUB):
        side = []
        if s + 1 < SSM_SUB:
            norm_piece(s + 1)
            side += before(s + 1)
        if s >= 1:
            side += after(s - 1)
        main = [(t, g) for t in range(tsub) for g in range(ngroup)]
        every = max(1, (len(main) - 1) // max(1, len(side)))
        token = None
        for n, (t, g) in enumerate(main):
            carry[g] = scan_piece(s, t, g, carry[g], token)
            token = side.pop(0)() if (n % every == every - 1 and side) else None
        for piece in side:
            piece()
    for piece in after(SSM_SUB - 1):
        piece()
    for g in range(ngroup):
        for n, k in enumerate(range(2 * SSM_SCAN_CHUNKS * g, 2 * SSM_SCAN_CHUNKS * (g + 1))):
            state_ref[:, LANES * k:LANES * (k + 1)] = carry[g][n]


def _ssm_discretize(log_dt, a_re, a_im, b_re, b_im, c_re, c_im):
    groups, state = a_re.shape
    nchunk = groups // SSM_CHUNK_GROUPS
    per_slab = LANES // SSM_CHUNK_CH
    dt = jnp.exp(log_dt)[:, None]
    mag = jnp.exp(a_re * dt)
    ang = a_im * dt
    abar_re = mag * jnp.cos(ang)
    abar_im = mag * jnp.sin(ang)
    den = a_re * a_re + a_im * a_im
    nr = abar_re - 1.0
    ni = abar_im
    zr = ((nr * a_re + ni * a_im) / den)[..., None]
    zi = ((ni * a_re - nr * a_im) / den)[..., None]
    bb_re = zr * b_re - zi * b_im
    bb_im = zr * b_im + zi * b_re
    eye = jnp.eye(SSM_CHUNK_GROUPS, dtype=F32)
    slot = jax.nn.one_hot(jnp.arange(nchunk) % per_slab, per_slab, dtype=F32)

    def b_block(bb):
        x = bb.transpose(0, 2, 1).reshape(nchunk, SSM_CHUNK_GROUPS, SSM_GROUP, state)
        return jnp.einsum("cgip,gh->cgihp", x, eye).reshape(nchunk, SSM_CHUNK_CH, LANES)

    bblk = jnp.concatenate([b_block(bb_re), b_block(bb_im)], axis=-1)
    bmat = jnp.einsum("cq,crn->cqrn", slot, bblk).reshape(nchunk, LANES, SSM_CHUNK_COLS)

    def c_block(cc):
        x = cc.transpose(0, 2, 1).reshape(nchunk, SSM_CHUNK_GROUPS, state, SSM_GROUP)
        return jnp.einsum("cgpo,gh->cgpho", x, eye).reshape(nchunk, LANES, SSM_CHUNK_CH)

    cblk = jnp.concatenate([c_block(c_re), -c_block(c_im)], axis=1)
    cmat = jnp.einsum("cq,ckr->ckqr", slot, cblk).reshape(nchunk, SSM_CHUNK_COLS, LANES)
    return (bmat.astype(BF16), abar_re.reshape(1, groups * state), abar_im.reshape(1, groups * state),
            cmat.astype(BF16))


def _ssm_layer(h3d, g, w_in, log_dt, a_re, a_im, b_re, b_im, c_re, c_im, d_skip, w_glu, b_glu, w_out, ts):
    seq, nb, d = h3d.shape
    width = d_skip.shape[-1]
    groups, state = a_re.shape
    assert nb == SUBLANES and state == SSM_STATE and width == groups * SSM_GROUP
    nchunk = groups // SSM_CHUNK_GROUPS
    assert nchunk % SSM_SCAN_CHUNKS == 0
    bmat, ar, ai, cmat = _ssm_discretize(log_dt, a_re, a_im, b_re, b_im, c_re, c_im)
    ncols = groups * state
    blk = pl.BlockSpec((ts, nb, d), lambda i: (i, 0, 0))
    kern = functools.partial(_ssm_kernel, ts=ts, nb=nb, width=width)
    return pl.pallas_call(
        kern,
        grid=(seq // ts,),
        in_specs=[blk, _const_spec((1, d)), _const_spec((d, 2 * width)),
                  _const_spec((nchunk, LANES, SSM_CHUNK_COLS)), _const_spec((1, ncols)), _const_spec((1, ncols)),
                  _const_spec((nchunk, SSM_CHUNK_COLS, LANES)), _const_spec((1, width)),
                  _const_spec((width, width)), _const_spec((1, width)), _const_spec((width, d))],
        out_specs=blk,
        out_shape=jax.ShapeDtypeStruct((seq, nb, d), F32),
        scratch_shapes=[pltpu.VMEM((ts * nb, 2 * ncols), F32), pltpu.VMEM((nb, 2 * ncols), F32),
                        pltpu.VMEM((2, nb, ncols), F32), pltpu.VMEM((ts * nb, d), BF16),
                        pltpu.VMEM((ts * nb, width), F32), pltpu.VMEM((ts * nb, width), BF16),
                        pltpu.VMEM((ts * nb, width), F32), pltpu.VMEM((ts * nb, width), BF16),
                        pltpu.VMEM((ts * nb, width), BF16)],
        compiler_params=_params(("arbitrary",)),
        name="ssm_layer",
    )(h3d, g.reshape(1, d), w_in.astype(BF16), bmat, ar, ai, cmat, d_skip.reshape(1, width),
      w_glu.astype(BF16), b_glu.reshape(1, width), w_out.astype(BF16))


def kernel(x, norm_g, a_w_in, a_conv_w, a_conv_b, a_ln_g, a_ln_b, a_w_out, b_w_in, b_f_bias, b_q_norm,
           b_k_norm, b_w_out, c_w_in, c_log_dt, c_a_re, c_a_im, c_b_re, c_b_im, c_c_re, c_c_im, c_d,
           c_w_glu, c_b_glu, c_w_out):
    bsz, seq, d = x.shape
    depth = norm_g.shape[0]
    tm = min(ROW_TILE, seq)
    tq = min(ATTN_TILE, seq)
    ts = min(SSM_STEPS, seq)
    h = x.reshape(bsz * seq, d)
    seq_major = False
    for layer in range(depth):
        kind, j = layer % 3, layer // 3
        next_major = (layer + 1 < depth) and ((layer + 1) % 3 == 2)
        if kind == 0:
            h = _conv_layer(h, bsz, seq, seq_major, next_major, norm_g[layer], a_w_in[j], a_conv_w[j],
                            a_conv_b[j], a_ln_g[j], a_ln_b[j], a_w_out[j], tm)
        elif kind == 1:
            assert not seq_major
            q, k, kc, vt, gate = _fox_proj(h, bsz, seq, norm_g[layer], b_w_in[j], b_f_bias[j],
                                           b_q_norm[j], b_k_norm[j], tq)
            oa, ob, ngroups = _fox_attn(q, k, kc, vt, gate, bsz, seq, tq)
            h = _out_proj(oa, ob, ngroups, b_w_out[j], h, bsz, seq, next_major, tm)
        else:
            assert seq_major
            h3 = _ssm_layer(h.reshape(seq, bsz, d), norm_g[layer], c_w_in[j], c_log_dt[j], c_a_re[j],
                            c_a_im[j], c_b_re[j], c_b_im[j], c_c_re[j], c_c_im[j], c_d[j], c_w_glu[j],
                            c_b_glu[j], c_w_out[j], ts)
            h = h3.reshape(seq, bsz * d)
            next_major = True
        seq_major = next_major
    if seq_major:
        h = h.reshape(seq, bsz, d).transpose(1, 0, 2)
    return h.reshape(bsz, seq, d)
```

```python
import functools
import math

import jax
import jax.numpy as jnp
from jax import lax
from jax.experimental import pallas as pl
from jax.experimental.pallas import tpu as pltpu

F32 = jnp.float32
BF16 = jnp.bfloat16

RMS_EPS = 1e-6
LN_EPS = 1e-5
CONV_K = 31
HEAD_DIM = 64
SSM_GROUP = 16
SSM_STATE = 64

LANES = 128
SUBLANES = 8
MXU_DIM = 256
VMEM_LIMIT_BYTES = 56 * 1024 * 1024

ROW_TILE = 512
ATTN_TILE = 256
SSM_STEPS = 64

ATTN_PAIRS = 4
ATTN_UNROLL = 4
HEADS_PER_SLAB = LANES // HEAD_DIM
DECAY_PARTS = 3
ONES_ROWS = 16
LOG2E = math.log2(math.e)
NT_DIMS = (((1,), (1,)), ((), ()))

CONV_HALO = 32
CONV_CHUNK = 32
CONV_ROWS = 128
CONV_STRIP = 128
CONV_SUB = 256


def _rms(x, g):
    ms = jnp.mean(x * x, axis=-1, keepdims=True)
    return x * lax.rsqrt(ms + RMS_EPS) * g


def _silu(x):
    return x * jax.nn.sigmoid(x)


def _const_spec(shape):
    zeros = (0,) * len(shape)
    return pl.BlockSpec(shape, lambda *_: zeros, pipeline_mode=pl.Buffered(1))


def _params(sem):
    return pltpu.CompilerParams(dimension_semantics=sem, vmem_limit_bytes=VMEM_LIMIT_BYTES)


def _conv_kernel(h_ref, g_ref, win_ref, cw_ref, cb_ref, lng_ref, lnb_ref, wout_ref, o_ref,
                 ubuf_ref, c_ref, v_ref, hn_ref, gate_ref, *, tm, width):
    i = pl.program_id(1)

    @pl.when(i == 0)
    def _():
        ubuf_ref[0:CONV_HALO, :] = jnp.zeros((CONV_HALO, width), F32)

    lng = lng_ref[...]
    lnb = lnb_ref[...]
    base = CONV_HALO - (CONV_K - 1)
    nsub = tm // CONV_SUB
    strips = [slice(c0, c0 + CONV_STRIP) for c0 in range(0, width, CONV_STRIP)]
    mstrips = [slice(c0, c0 + MXU_DIM) for c0 in range(0, width, MXU_DIM)]

    def norm_piece(s):
        rows = slice(CONV_SUB * s, CONV_SUB * (s + 1))
        hn_ref[rows, :] = _rms(h_ref[rows, :], g_ref[...]).astype(BF16)

    def proj_piece(s, cols):
        rows = slice(CONV_SUB * s, CONV_SUB * (s + 1))
        hn = hn_ref[rows, :]
        part = [jnp.dot(hn, win_ref[:, n * width + cols.start:n * width + cols.stop], preferred_element_type=F32)
                for n in range(3)]
        ubuf_ref[CONV_HALO + rows.start:CONV_HALO + rows.stop, cols] = part[0] * jax.nn.sigmoid(part[1])
        gate_ref[rows, cols] = part[2]
        return gate_ref[rows.start:rows.start + SUBLANES, cols.start:cols.start + LANES]

    def conv_piece(r0, cols, token=None):
        acc = jnp.broadcast_to(cb_ref[:, cols], (CONV_ROWS, CONV_STRIP))
        for r in range(SUBLANES):
            nrows = CONV_ROWS + (SUBLANES if r else 0)
            part = None
            for a in range((base + CONV_K - 1) // SUBLANES + 1):
                k = SUBLANES * a + r - base
                if 0 <= k < CONV_K:
                    row = r0 + SUBLANES * a
                    term = cw_ref[k:k + 1, cols] * ubuf_ref[row:row + nrows, cols]
                    part = term if part is None else part + term
            acc = acc + part[r:r + CONV_ROWS, :]
        c_ref[r0:r0 + CONV_ROWS, cols] = acc
        if token is not None:
            tile = (slice(r0, r0 + SUBLANES), slice(cols.start, cols.start + LANES))
            c_ref[tile] = jnp.where(i < 0, token, c_ref[tile])

    def post_piece(r0):
        rows = slice(r0, r0 + CONV_CHUNK)
        acc = c_ref[rows, :]
        mu = jnp.mean(acc, axis=-1, keepdims=True)
        cen = acc - mu
        var = jnp.mean(cen * cen, axis=-1, keepdims=True)
        y = cen * lax.rsqrt(var + LN_EPS) * lng + lnb
        v_ref[rows, :] = (_silu(y) * _silu(gate_ref[rows, :])).astype(BF16)

    def out_piece(s, cols):
        rows = slice(CONV_SUB * s, CONV_SUB * (s + 1))
        o_ref[rows, cols] = h_ref[rows, cols] + jnp.dot(v_ref[rows, :], wout_ref[:, cols],
                                                        preferred_element_type=F32)
        return o_ref[rows.start:rows.start + SUBLANES, cols.start:cols.start + LANES]

    norm_piece(0)
    for cols in mstrips:
        proj_piece(0, cols)
    for s in range(nsub):
        side = []
        if s + 1 < nsub:
            norm_piece(s + 1)
            side += [functools.partial(proj_piece, s + 1, cols) for cols in mstrips]
        if s >= 1:
            side += [functools.partial(out_piece, s - 1, cols) for cols in mstrips]
        main = [functools.partial(conv_piece, r0, cols)
                for r0 in range(CONV_SUB * s, CONV_SUB * (s + 1), CONV_ROWS) for cols in strips]
        every = max(1, (len(main) - 1) // max(1, len(side)))
        token = None
        for n, piece in enumerate(main):
            piece(token=token)
            token = side.pop(0)() if (n % every == every - 1 and side) else None
        for piece in side:
            piece()
        for r0 in range(CONV_SUB * s, CONV_SUB * (s + 1), CONV_CHUNK):
            post_piece(r0)
    for cols in mstrips:
        out_piece(nsub - 1, cols)

    ubuf_ref[0:CONV_HALO, :] = ubuf_ref[tm:tm + CONV_HALO, :]


def _row_spec(tm, d, nt, seq_major):
    if seq_major:
        return pl.BlockSpec((tm, d), lambda b, i: (i, b))
    return pl.BlockSpec((tm, d), lambda b, i: (b * nt + i, 0))


def _conv_layer(h2d, bsz, seq, seq_major_in, seq_major_out, g, w_in, conv_w, conv_b, ln_g, ln_b, w_out, tm):
    d = g.shape[-1]
    width = conv_w.shape[-1]
    nt = seq // tm
    out_shape = (seq, bsz * d) if seq_major_out else (bsz * seq, d)
    kern = functools.partial(_conv_kernel, tm=tm, width=width)
    return pl.pallas_call(
        kern,
        grid=(bsz, nt),
        in_specs=[
            _row_spec(tm, d, nt, seq_major_in),
            _const_spec((1, d)),
            _const_spec((d, 3 * width)),
            _const_spec((CONV_K, width)),
            _const_spec((1, width)),
            _const_spec((1, width)),
            _const_spec((1, width)),
            _const_spec((width, d)),
        ],
        out_specs=_row_spec(tm, d, nt, seq_major_out),
        out_shape=jax.ShapeDtypeStruct(out_shape, F32),
        scratch_shapes=[pltpu.VMEM((tm + CONV_HALO, width), F32), pltpu.VMEM((tm, width), F32),
                        pltpu.VMEM((tm, width), BF16), pltpu.VMEM((tm, d), BF16), pltpu.VMEM((tm, width), F32)],
        compiler_params=_params(("arbitrary", "arbitrary")),
        name="conv_layer",
    )(h2d, g.reshape(1, d), w_in.astype(BF16), conv_w, conv_b.reshape(1, width),
      ln_g.reshape(1, width), ln_b.reshape(1, width), w_out.astype(BF16))


def _split3(x):
    p1 = x.astype(BF16)
    r1 = x - p1.astype(F32)
    p2 = r1.astype(BF16)
    r2 = r1 - p2.astype(F32)
    return p1, p2, r2.astype(BF16)


def _fox_proj_kernel(h_ref, g_ref, wq_ref, wk_ref, wvt_ref, wg_ref, wf_ref, fb_ref, qg_ref, kg_ref,
                     seg_ref, tri_ref, place_ref, q_ref, k_ref, kc_ref, vt_ref, gate_ref, carry_ref,
                     *, tm, width):
    i = pl.program_id(1)
    heads = width // HEAD_DIM

    @pl.when(i == 0)
    def _():
        carry_ref[...] = jnp.zeros(carry_ref.shape, F32)

    hn = _rms(h_ref[...], g_ref[...]).astype(BF16)
    seg = seg_ref[...]

    def head_norm(x, gain):
        x2 = x * x
        hi = x2.astype(BF16)
        lo = (x2 - hi.astype(F32)).astype(BF16)
        parts = []
        for s in range(width // MXU_DIM):
            sl = slice(MXU_DIM * s, MXU_DIM * (s + 1))
            parts.append(jnp.dot(hi[:, sl], seg, preferred_element_type=F32)
                         + jnp.dot(lo[:, sl], seg, preferred_element_type=F32))
        ss = jnp.concatenate(parts, axis=-1)
        return x * lax.rsqrt(ss * (1.0 / HEAD_DIM) + RMS_EPS) * gain

    zq = jnp.dot(hn, wq_ref[...], preferred_element_type=F32)
    q_ref[...] = (head_norm(zq, qg_ref[...]) * (LOG2E / math.sqrt(HEAD_DIM))).astype(BF16)
    zk = jnp.dot(hn, wk_ref[...], preferred_element_type=F32)
    k_ref[...] = head_norm(zk, kg_ref[...]).astype(BF16)
    vt_ref[...] = lax.dot_general(wvt_ref[...], hn, NT_DIMS, preferred_element_type=F32).astype(BF16)
    gate_ref[...] = jnp.dot(hn, wg_ref[...], preferred_element_type=F32)

    zf = jnp.dot(hn, wf_ref[...], preferred_element_type=F32) + fb_ref[...]
    logf = jnp.minimum(zf, 0.0) - jnp.log1p(jnp.exp(-jnp.abs(zf)))
    tri = tri_ref[...]
    p1, p2, p3 = _split3(logf)
    local = (jnp.dot(tri, p1, preferred_element_type=F32)
             + jnp.dot(tri, p2, preferred_element_type=F32)
             + jnp.dot(tri, p3, preferred_element_type=F32))
    cum = local + carry_ref[0:1, :]
    carry_ref[...] = jnp.broadcast_to(cum[tm - 1:tm, :], carry_ref.shape)
    d1, d2, d3 = _split3(cum * (-LOG2E))
    lane = lax.broadcasted_iota(jnp.int32, (tm, LANES), 1)
    parts = jnp.where(lane < heads, d1, jnp.where(lane < 2 * heads, d2, d3))
    kc_ref[...] = jnp.dot(parts, place_ref[...], preferred_element_type=F32).astype(BF16)


def _fox_proj(h2d, bsz, seq, g, w_in, f_bias, q_g, k_g, tm):
    d = g.shape[-1]
    width = (w_in.shape[-1] // (4 * HEAD_DIM + 1)) * HEAD_DIM
    heads = width // HEAD_DIM
    assert DECAY_PARTS * heads <= LANES
    nt = seq // tm
    wb = w_in.astype(BF16)
    wq, wk, wv, wg = (wb[:, n * width:(n + 1) * width] for n in range(4))
    wf = jnp.pad(jnp.tile(wb[:, 4 * width:], (1, DECAY_PARTS)), ((0, 0), (0, LANES - DECAY_PARTS * heads)))
    fb = jnp.pad(jnp.tile(f_bias, DECAY_PARTS), (0, LANES - DECAY_PARTS * heads)).reshape(1, LANES)
    seg_id = jnp.arange(MXU_DIM) // HEAD_DIM
    seg = (seg_id[:, None] == seg_id[None, :]).astype(BF16)
    tri = (jnp.arange(tm)[:, None] >= jnp.arange(tm)[None, :]).astype(BF16)
    src = jnp.arange(LANES)
    part, head = src // heads, src % heads
    dst = (head // HEADS_PER_SLAB) * LANES + DECAY_PARTS * (head % HEADS_PER_SLAB) + part
    place = ((dst[:, None] == jnp.arange(width)[None, :]) & (src[:, None] < DECAY_PARTS * heads)).astype(BF16)
    row = lambda n: pl.BlockSpec((tm, n), lambda b, i: (b * nt + i, 0))
    kern = functools.partial(_fox_proj_kernel, tm=tm, width=width)
    m = bsz * seq
    return pl.pallas_call(
        kern,
        grid=(bsz, nt),
        in_specs=[row(d), _const_spec((1, d)), _const_spec((d, width)), _const_spec((d, width)),
                  _const_spec((width, d)), _const_spec((d, width)),
                  _const_spec((d, LANES)), _const_spec((1, LANES)), _const_spec((1, width)), _const_spec((1, width)),
                  _const_spec((MXU_DIM, MXU_DIM)), _const_spec((tm, tm)), _const_spec((LANES, width))],
        out_specs=[row(width), row(width), row(width),
                   pl.BlockSpec((None, None, width, tm), lambda b, i: (b, i, 0, 0)), row(width)],
        out_shape=[jax.ShapeDtypeStruct((m, width), BF16)] * 3
        + [jax.ShapeDtypeStruct((bsz, nt, width, tm), BF16), jax.ShapeDtypeStruct((m, width), F32)],
        scratch_shapes=[pltpu.VMEM((SUBLANES, LANES), F32)],
        compiler_params=_params(("arbitrary", "arbitrary")),
        name="fox_proj",
    )(h2d, g.reshape(1, d), wq, wk, wv.T, wg, wf, fb,
      jnp.tile(q_g, heads).reshape(1, width), jnp.tile(k_g, heads).reshape(1, width), seg, tri, place)


def _attn_kernel(q_ref, k_ref, kc_ref, vt_ref, gate_a_ref, gate_b_ref, oa_ref, ob_ref,
                 qt_ref, sta_ref, stb_ref, acc_ref, mx_ref, cm_ref, *, tq, npairs, nq):
    i = pl.program_id(2)
    half = npairs // 2
    nch = HEADS_PER_SLAB * half
    st_refs = (sta_ref, stb_ref)
    lane = lax.broadcasted_iota(jnp.int32, (tq, LANES), 1)
    on_or_below_diagonal = (lax.broadcasted_iota(jnp.int32, (tq, tq), 0)
                            <= lax.broadcasted_iota(jnp.int32, (tq, tq), 1))

    def band(lo, hi):
        return jnp.where(lane < hi, 1.0, 0.0) * jnp.where(lane < lo, 0.0, 1.0)

    ones = jnp.ones((ONES_ROWS, tq), BF16)

    def score_piece(hf, n, j, diagonal):
        c = nch * hf + n
        slab = slice(LANES * (c // HEADS_PER_SLAB), LANES * (c // HEADS_PER_SLAB + 1))
        rows = pl.ds(pl.multiple_of(j * tq, tq), tq)
        k_aug = jnp.concatenate([k_ref[rows, slab], kc_ref[rows, slab]], axis=-1)
        st = jnp.dot(k_aug, qt_ref[c], preferred_element_type=F32)
        if diagonal:
            st = jnp.where(on_or_below_diagonal, st, -jnp.inf)
        st_refs[hf][n, j] = st
        mx_ref[c] = jnp.maximum(mx_ref[c], jnp.max(st.reshape(tq // SUBLANES, SUBLANES, tq), axis=0))

    def acc_piece(hf, n, j):
        c = nch * hf + n
        v0 = HEAD_DIM * c
        v_aug = jnp.concatenate([vt_ref[j, v0:v0 + HEAD_DIM, :], ones], axis=0)
        pt = jnp.exp2(st_refs[hf][n, j] - cm_ref[c]).astype(BF16)
        acc_ref[c] += jnp.dot(v_aug, pt, preferred_element_type=F32)

    def scores(hf, j0, count, diagonal):
        for u in range(count):
            for n in range(nch):
                score_piece(hf, n, j0 + u, diagonal)

    def accumulate(hf, j0, count):
        for u in range(count):
            for n in range(nch):
                acc_piece(hf, n, j0 + u)

    def both(score_half, j0, count, diagonal):
        for u in range(count):
            for n in range(nch):
                score_piece(score_half, n, j0 + u, diagonal)
                acc_piece(1 - score_half, n, j0 + u)

    def grouped(fn, nblocks):
        ngroups = nblocks // ATTN_UNROLL

        def body(g, carry):
            fn(g * ATTN_UNROLL, ATTN_UNROLL)
            return carry
        lax.fori_loop(0, ngroups, body, 0)
        for rem in range(1, ATTN_UNROLL):
            @pl.when(nblocks - ngroups * ATTN_UNROLL == rem)
            def _(rem=rem):
                fn(ngroups * ATTN_UNROLL, rem)

    def reset(hf, ref, value):
        ref[nch * hf:nch * (hf + 1)] = jnp.full((nch,) + ref.shape[1:], value, F32)

    def column_max(hf):
        for c in range(nch * hf, nch * (hf + 1)):
            cm_ref[c] = jnp.max(mx_ref[c], axis=0, keepdims=True)

    def finish(hf, gate_ref, o_ref):
        for p in range(half):
            halves = []
            for e in range(HEADS_PER_SLAB):
                acc = acc_ref[nch * hf + HEADS_PER_SLAB * p + e]
                halves.append(acc[:HEAD_DIM, :] / acc[HEAD_DIM:HEAD_DIM + 1, :])
            o = jnp.concatenate(halves, axis=0).T
            slab = slice(LANES * p, LANES * (p + 1))
            o_ref[:, slab] = (o * _silu(gate_ref[:, slab])).astype(BF16)

    reset(1, acc_ref, 0.0)

    @pl.when(i < nq)
    def _():
        for p in range(npairs):
            q = q_ref[:, LANES * p:LANES * (p + 1)].astype(F32)
            for e in range(HEADS_PER_SLAB):
                q_aug = jnp.concatenate(
                    [q * band(HEAD_DIM * e, HEAD_DIM * (e + 1)), band(DECAY_PARTS * e, DECAY_PARTS * (e + 1))],
                    axis=-1)
                qt_ref[HEADS_PER_SLAB * p + e] = q_aug.T.astype(BF16)
        reset(0, mx_ref, -jnp.inf)

        grouped(lambda j0, count: both(0, j0, count, False), i)
        scores(0, i, 1, True)

    @pl.when(i == nq)
    def _():
        grouped(functools.partial(accumulate, 1), nq)

    @pl.when(i >= 1)
    def _():
        finish(1, gate_b_ref, ob_ref)

    @pl.when(i < nq)
    def _():
        column_max(0)
        reset(0, acc_ref, 0.0)
        reset(1, mx_ref, -jnp.inf)

        grouped(lambda j0, count: both(1, j0, count, False), i)
        both(1, i, 1, True)
        finish(0, gate_a_ref, oa_ref)
        column_max(1)


def _fox_attn(q, k, kc, vt, gate, bsz, seq, tq):
    m, width = q.shape
    nq = seq // tq
    npairs = min(ATTN_PAIRS, width // LANES)
    assert npairs % 2 == 0
    nchains = HEADS_PER_SLAB * npairs
    cols = npairs * LANES
    hcols = cols // 2
    cur = lambda i: jnp.minimum(i, nq - 1)
    prev = lambda i: jnp.maximum(i - 1, 0)
    kspec = pl.BlockSpec((seq, cols), lambda b, p, i: (b, p))
    oa, ob = pl.pallas_call(
        functools.partial(_attn_kernel, tq=tq, npairs=npairs, nq=nq),
        grid=(bsz, width // cols, nq + 1),
        in_specs=[pl.BlockSpec((tq, cols), lambda b, p, i: (b * nq + cur(i), p)), kspec, kspec,
                  pl.BlockSpec((None, nq, cols, tq), lambda b, p, i: (b, 0, p, 0)),
                  pl.BlockSpec((tq, hcols), lambda b, p, i: (b * nq + cur(i), 2 * p)),
                  pl.BlockSpec((tq, hcols), lambda b, p, i: (b * nq + prev(i), 2 * p + 1))],
        out_specs=[pl.BlockSpec((tq, hcols), lambda b, p, i: (b * nq + cur(i), p)),
                   pl.BlockSpec((tq, hcols), lambda b, p, i: (b * nq + prev(i), p))],
        out_shape=[jax.ShapeDtypeStruct((m, width // 2), BF16)] * 2,
        scratch_shapes=[pltpu.VMEM((nchains, 2 * LANES, tq), BF16),
                        pltpu.VMEM((nchains // 2, nq, tq, tq), F32),
                        pltpu.VMEM((nchains // 2, nq, tq, tq), F32),
                        pltpu.VMEM((nchains, HEAD_DIM + ONES_ROWS, tq), F32),
                        pltpu.VMEM((nchains, SUBLANES, tq), F32),
                        pltpu.VMEM((nchains, 1, tq), F32)],
        compiler_params=_params(("arbitrary", "arbitrary", "arbitrary")),
        name="fox_attn",
    )(q, k, kc, vt, gate, gate)
    return oa, ob, width // cols


def _out_proj_kernel(xa_ref, xb_ref, wa_ref, wb_ref, h_ref, o_ref):
    o_ref[...] = (h_ref[...] + jnp.dot(xa_ref[...], wa_ref[...], preferred_element_type=F32)
                  + jnp.dot(xb_ref[...], wb_ref[...], preferred_element_type=F32))


def _out_proj(xa, xb, ngroups, w_out, h2d, bsz, seq, seq_major_out, tm):
    width, d = w_out.shape
    nt = seq // tm
    w = w_out.astype(BF16).reshape(ngroups, 2, width // (2 * ngroups), d)
    wa, wb = (w[:, n].reshape(width // 2, d) for n in range(2))
    out_shape = (seq, bsz * d) if seq_major_out else (bsz * seq, d)
    xspec = _row_spec(tm, width // 2, nt, False)
    return pl.pallas_call(
        _out_proj_kernel,
        grid=(bsz, nt),
        in_specs=[xspec, xspec, _const_spec((width // 2, d)), _const_spec((width // 2, d)),
                  _row_spec(tm, d, nt, False)],
        out_specs=_row_spec(tm, d, nt, seq_major_out),
        out_shape=jax.ShapeDtypeStruct(out_shape, F32),
        compiler_params=_params(("arbitrary", "arbitrary")),
        name="fox_out",
    )(xa, xb, wa, wb, h2d)


SSM_CHUNK_GROUPS = LANES // SSM_STATE
SSM_CHUNK_COLS = 2 * LANES
SSM_CHUNK_CH = SSM_CHUNK_GROUPS * SSM_GROUP
SSM_SCAN_CHUNKS = 8
SSM_SUB = 2


def _ssm_kernel(h_ref, g_ref, win_ref, bmat_ref, ar_ref, ai_ref, cmat_ref, d_ref, wglu_ref, bglu_ref,
                wout_ref, o_ref, bu_ref, state_ref, coef_ref, hn_ref, u_ref, ub_ref, gate_ref, gl_ref, v_ref,
                hin_ref, hout_ref, *, ts, nb, width):
    i = pl.program_id(0)
    rows = ts * nb
    rs, tsub = rows // SSM_SUB, ts // SSM_SUB
    nchunk = width // SSM_CHUNK_CH
    per_slab = LANES // SSM_CHUNK_CH
    ngroup = nchunk // SSM_SCAN_CHUNKS
    strips = [slice(c0, c0 + MXU_DIM) for c0 in range(0, width, MXU_DIM)]

    @pl.when(i == 0)
    def _():
        state_ref[...] = jnp.zeros(state_ref.shape, F32)
        coef_ref[0] = jnp.broadcast_to(ar_ref[...], coef_ref.shape[1:])
        coef_ref[1] = jnp.broadcast_to(ai_ref[...], coef_ref.shape[1:])

    def sub_rows(s):
        return slice(rs * s, rs * (s + 1))

    def batch_rows(s, b):
        return pl.ds(rs * s + b, tsub, stride=nb)

    def norm_piece(s):
        t = slice(tsub * s, tsub * (s + 1))
        for b in range(nb):
            for j in range(width // LANES):
                hin_ref[j, batch_rows(s, b), :] = h_ref[t, width * b + LANES * j:width * b + LANES * (j + 1)]
        h = jnp.concatenate([hin_ref[j, sub_rows(s), :] for j in range(width // LANES)], axis=-1)
        hn_ref[sub_rows(s), :] = _rms(h, g_ref[...]).astype(BF16)

    def in_piece(s, cols):
        r = sub_rows(s)
        hn = hn_ref[r, :]
        u = jnp.dot(hn, win_ref[:, cols], preferred_element_type=F32)
        u_ref[r, cols] = u
        ub_ref[r, cols] = u.astype(BF16)
        gate_ref[r, cols] = jnp.dot(hn, win_ref[:, width + cols.start:width + cols.stop],
                                    preferred_element_type=F32)
        return gate_ref[r.start:r.start + nb, cols.start:cols.start + LANES]

    def bproj_piece(s, c):
        r = sub_rows(s)
        slab = c // per_slab
        ccols = slice(SSM_CHUNK_COLS * c, SSM_CHUNK_COLS * (c + 1))
        bu_ref[r, ccols] = jnp.dot(ub_ref[r, LANES * slab:LANES * (slab + 1)], bmat_ref[c],
                                   preferred_element_type=F32)
        return bu_ref[r.start:r.start + nb, ccols.start:ccols.start + LANES]

    def scan_piece(s, t, g, carry, token):
        r = slice(rs * s + nb * t, rs * s + nb * (t + 1))
        new = []
        for n, c in enumerate(range(SSM_SCAN_CHUNKS * g, SSM_SCAN_CHUNKS * (g + 1))):
            xr, xi = carry[2 * n], carry[2 * n + 1]
            lanes = slice(LANES * c, LANES * (c + 1))
            ar, ai = coef_ref[0, :, lanes], coef_ref[1, :, lanes]
            re_cols = slice(SSM_CHUNK_COLS * c, SSM_CHUNK_COLS * c + LANES)
            im_cols = slice(SSM_CHUNK_COLS * c + LANES, SSM_CHUNK_COLS * (c + 1))
            nr = ar * xr - ai * xi + bu_ref[r, re_cols]
            ni = ar * xi + ai * xr + bu_ref[r, im_cols]
            if n == 0 and token is not None:
                nr = jnp.where(i < 0, token, nr)
            bu_ref[r, re_cols] = nr
            bu_ref[r, im_cols] = ni
            new += [nr, ni]
        return new

    def cproj_piece(s, slab):
        r = sub_rows(s)
        cols = slice(LANES * slab, LANES * (slab + 1))
        acc = None
        for c in range(per_slab * slab, per_slab * (slab + 1)):
            xc = bu_ref[r, SSM_CHUNK_COLS * c:SSM_CHUNK_COLS * (c + 1)].astype(BF16)
            part = jnp.dot(xc, cmat_ref[c], preferred_element_type=F32)
            acc = part if acc is None else acc + part
        gl = jax.nn.gelu(acc + d_ref[:, cols] * u_ref[r, cols])
        u_ref[r, cols] = gl
        gl_ref[r, cols] = gl.astype(BF16)
        return u_ref[r.start:r.start + nb, cols]

    def glu_piece(s, cols):
        r = sub_rows(s)
        t2 = jnp.dot(gl_ref[r, :], wglu_ref[:, cols], preferred_element_type=F32) + bglu_ref[:, cols]
        y = u_ref[r, cols] * jax.nn.sigmoid(t2)
        v_ref[r, cols] = (y * _silu(gate_ref[r, cols])).astype(BF16)
        return None

    def out_piece(s, cols):
        r = sub_rows(s)
        t = slice(tsub * s, tsub * (s + 1))
        planes = range(cols.start // LANES, cols.stop // LANES)
        h = jnp.concatenate([hin_ref[j, r, :] for j in planes], axis=-1)
        out = h + jnp.dot(v_ref[r, :], wout_ref[:, cols], preferred_element_type=F32)
        for n, j in enumerate(planes):
            hout_ref[j, r, :] = out[:, LANES * n:LANES * (n + 1)]
        for b in range(nb):
            for j in planes:
                o_ref[t, width * b + LANES * j:width * b + LANES * (j + 1)] = hout_ref[j, batch_rows(s, b), :]
        return o_ref[t.start:t.start + nb, cols.start:cols.start + LANES]

    def before(s):
        return ([functools.partial(in_piece, s, cols) for cols in strips]
                + [functools.partial(bproj_piece, s, c) for c in range(nchunk)])

    def after(s):
        return ([functools.partial(cproj_piece, s, slab) for slab in range(width // LANES)]
                + [functools.partial(glu_piece, s, cols) for cols in strips]
                + [functools.partial(out_piece, s, cols) for cols in strips])

    norm_piece(0)
    for piece in before(0):
        piece()
    carry = [[state_ref[:, LANES * k:LANES * (k + 1)]
              for k in range(2 * SSM_SCAN_CHUNKS * g, 2 * SSM_SCAN_CHUNKS * (g + 1))] for g in range(ngroup)]
    for s in range(SSM_SUB):
        side = []
        if s + 1 < SSM_SUB:
            norm_piece(s + 1)
            side += before(s + 1)
        if s >= 1:
            side += after(s - 1)
        main = [(t, g) for t in range(tsub) for g in range(ngroup)]
        every = max(1, (len(main) - 1) // max(1, len(side)))
        token = None
        for n, (t, g) in enumerate(main):
            carry[g] = scan_piece(s, t, g, carry[g], token)
            token = side.pop(0)() if (n % every == every - 1 and side) else None
        for piece in side:
            piece()
    for piece in after(SSM_SUB - 1):
        piece()
    for g in range(ngroup):
        for n, k in enumerate(range(2 * SSM_SCAN_CHUNKS * g, 2 * SSM_SCAN_CHUNKS * (g + 1))):
            state_ref[:, LANES * k:LANES * (k + 1)] = carry[g][n]


def _ssm_discretize(log_dt, a_re, a_im, b_re, b_im, c_re, c_im):
    groups, state = a_re.shape
    nchunk = groups // SSM_CHUNK_GROUPS
    per_slab = LANES // SSM_CHUNK_CH
    dt = jnp.exp(log_dt)[:, None]
    mag = jnp.exp(a_re * dt)
    ang = a_im * dt
    abar_re = mag * jnp.cos(ang)
    abar_im = mag * jnp.sin(ang)
    den = a_re * a_re + a_im * a_im
    nr = abar_re - 1.0
    ni = abar_im
    zr = ((nr * a_re + ni * a_im) / den)[..., None]
    zi = ((ni * a_re - nr * a_im) / den)[..., None]
    bb_re = zr * b_re - zi * b_im
    bb_im = zr * b_im + zi * b_re
    eye = jnp.eye(SSM_CHUNK_GROUPS, dtype=F32)
    slot = jax.nn.one_hot(jnp.arange(nchunk) % per_slab, per_slab, dtype=F32)

    def b_block(bb):
        x = bb.transpose(0, 2, 1).reshape(nchunk, SSM_CHUNK_GROUPS, SSM_GROUP, state)
        return jnp.einsum("cgip,gh->cgihp", x, eye).reshape(nchunk, SSM_CHUNK_CH, LANES)

    bblk = jnp.concatenate([b_block(bb_re), b_block(bb_im)], axis=-1)
    bmat = jnp.einsum("cq,crn->cqrn", slot, bblk).reshape(nchunk, LANES, SSM_CHUNK_COLS)

    def c_block(cc):
        x = cc.transpose(0, 2, 1).reshape(nchunk, SSM_CHUNK_GROUPS, state, SSM_GROUP)
        return jnp.einsum("cgpo,gh->cgpho", x, eye).reshape(nchunk, LANES, SSM_CHUNK_CH)

    cblk = jnp.concatenate([c_block(c_re), -c_block(c_im)], axis=1)
    cmat = jnp.einsum("cq,ckr->ckqr", slot, cblk).reshape(nchunk, SSM_CHUNK_COLS, LANES)
    return (bmat.astype(BF16), abar_re.reshape(1, groups * state), abar_im.reshape(1, groups * state),
            cmat.astype(BF16))


def _ssm_layer(h2d, nb, g, w_in, log_dt, a_re, a_im, b_re, b_im, c_re, c_im, d_skip, w_glu, b_glu, w_out, ts):
    seq, d = h2d.shape[0], h2d.shape[1] // nb
    width = d_skip.shape[-1]
    groups, state = a_re.shape
    assert nb == SUBLANES and state == SSM_STATE and width == groups * SSM_GROUP
    nchunk = groups // SSM_CHUNK_GROUPS
    assert nchunk % SSM_SCAN_CHUNKS == 0
    bmat, ar, ai, cmat = _ssm_discretize(log_dt, a_re, a_im, b_re, b_im, c_re, c_im)
    ncols = groups * state
    blk = pl.BlockSpec((ts, nb * d), lambda i: (i, 0))
    kern = functools.partial(_ssm_kernel, ts=ts, nb=nb, width=width)
    return pl.pallas_call(
        kern,
        grid=(seq // ts,),
        in_specs=[blk, _const_spec((1, d)), _const_spec((d, 2 * width)),
                  _const_spec((nchunk, LANES, SSM_CHUNK_COLS)), _const_spec((1, ncols)), _const_spec((1, ncols)),
                  _const_spec((nchunk, SSM_CHUNK_COLS, LANES)), _const_spec((1, width)),
                  _const_spec((width, width)), _const_spec((1, width)), _const_spec((width, d))],
        out_specs=blk,
        out_shape=jax.ShapeDtypeStruct((seq, nb * d), F32),
        scratch_shapes=[pltpu.VMEM((ts * nb, 2 * ncols), F32), pltpu.VMEM((nb, 2 * ncols), F32),
                        pltpu.VMEM((2, nb, ncols), F32), pltpu.VMEM((ts * nb, d), BF16),
                        pltpu.VMEM((ts * nb, width), F32), pltpu.VMEM((ts * nb, width), BF16),
                        pltpu.VMEM((ts * nb, width), F32), pltpu.VMEM((ts * nb, width), BF16),
                        pltpu.VMEM((ts * nb, width), BF16),
                        pltpu.VMEM((d // LANES, ts * nb, LANES), F32), pltpu.VMEM((d // LANES, ts * nb, LANES), F32)],
        compiler_params=_params(("arbitrary",)),
        name="ssm_layer",
    )(h2d, g.reshape(1, d), w_in.astype(BF16), bmat, ar, ai, cmat, d_skip.reshape(1, width),
      w_glu.astype(BF16), b_glu.reshape(1, width), w_out.astype(BF16))


def kernel(x, norm_g, a_w_in, a_conv_w, a_conv_b, a_ln_g, a_ln_b, a_w_out, b_w_in, b_f_bias, b_q_norm,
           b_k_norm, b_w_out, c_w_in, c_log_dt, c_a_re, c_a_im, c_b_re, c_b_im, c_c_re, c_c_im, c_d,
           c_w_glu, c_b_glu, c_w_out):
    bsz, seq, d = x.shape
    depth = norm_g.shape[0]
    tm = min(ROW_TILE, seq)
    tq = min(ATTN_TILE, seq)
    ts = min(SSM_STEPS, seq)
    h = x.reshape(bsz * seq, d)
    seq_major = False
    for layer in range(depth):
        kind, j = layer % 3, layer // 3
        next_major = (layer + 1 < depth) and ((layer + 1) % 3 == 2)
        if kind == 0:
            h = _conv_layer(h, bsz, seq, seq_major, next_major, norm_g[layer], a_w_in[j], a_conv_w[j],
                            a_conv_b[j], a_ln_g[j], a_ln_b[j], a_w_out[j], tm)
        elif kind == 1:
            assert not seq_major
            q, k, kc, vt, gate = _fox_proj(h, bsz, seq, norm_g[layer], b_w_in[j], b_f_bias[j],
                                           b_q_norm[j], b_k_norm[j], tq)
            oa, ob, ngroups = _fox_attn(q, k, kc, vt, gate, bsz, seq, tq)
            h = _out_proj(oa, ob, ngroups, b_w_out[j], h, bsz, seq, next_major, tm)
        else:
            assert seq_major
            h = _ssm_layer(h, bsz, norm_g[layer], c_w_in[j], c_log_dt[j], c_a_re[j],
                           c_a_im[j], c_b_re[j], c_b_im[j], c_c_re[j], c_c_im[j], c_d[j], c_w_glu[j],
                           c_b_glu[j], c_w_out[j], ts)
            next_major = True
        seq_major = next_major
    if seq_major:
        h = h.reshape(seq, bsz, d).transpose(1, 0, 2)
    return h.reshape(bsz, seq, d)
```

```python
import functools
import math

import jax
import jax.numpy as jnp
from jax import lax
from jax.experimental import pallas as pl
from jax.experimental.pallas import tpu as pltpu

F32 = jnp.float32
BF16 = jnp.bfloat16

RMS_EPS = 1e-6
LN_EPS = 1e-5
CONV_K = 31
HEAD_DIM = 64
SSM_GROUP = 16
SSM_STATE = 64

LANES = 128
SUBLANES = 8
MXU_DIM = 256
VMEM_LIMIT_BYTES = 56 * 1024 * 1024

ROW_TILE = 512
ATTN_TILE = 256
SSM_STEPS = 64

ATTN_PAIRS = 4
ATTN_UNROLL = 4
HEADS_PER_SLAB = LANES // HEAD_DIM
DECAY_PARTS = 3
ONES_ROWS = 16
LOG2E = math.log2(math.e)
NT_DIMS = (((1,), (1,)), ((), ()))

CONV_HALO = 32
CONV_CHUNK = 32
CONV_ROWS = 128
CONV_STRIP = 128
CONV_SUB = 256


def _rms(x, g):
    ms = jnp.mean(x * x, axis=-1, keepdims=True)
    return x * lax.rsqrt(ms + RMS_EPS) * g


def _silu(x):
    return x * jax.nn.sigmoid(x)


def _const_spec(shape):
    zeros = (0,) * len(shape)
    return pl.BlockSpec(shape, lambda *_: zeros, pipeline_mode=pl.Buffered(1))


def _params(sem):
    return pltpu.CompilerParams(dimension_semantics=sem, vmem_limit_bytes=VMEM_LIMIT_BYTES)


def _conv_kernel(h_ref, g_ref, win_ref, cw_ref, cb_ref, lng_ref, lnb_ref, wout_ref, o_ref,
                 ubuf_ref, c_ref, v_ref, hn_ref, gate_ref, *, tm, width):
    i = pl.program_id(1)

    @pl.when(i == 0)
    def _():
        ubuf_ref[0:CONV_HALO, :] = jnp.zeros((CONV_HALO, width), F32)

    lng = lng_ref[...]
    lnb = lnb_ref[...]
    base = CONV_HALO - (CONV_K - 1)
    nsub = tm // CONV_SUB
    strips = [slice(c0, c0 + CONV_STRIP) for c0 in range(0, width, CONV_STRIP)]
    mstrips = [slice(c0, c0 + MXU_DIM) for c0 in range(0, width, MXU_DIM)]

    def norm_piece(s):
        rows = slice(CONV_SUB * s, CONV_SUB * (s + 1))
        hn_ref[rows, :] = _rms(h_ref[rows, :], g_ref[...]).astype(BF16)

    def proj_piece(s, cols):
        rows = slice(CONV_SUB * s, CONV_SUB * (s + 1))
        hn = hn_ref[rows, :]
        part = [jnp.dot(hn, win_ref[:, n * width + cols.start:n * width + cols.stop], preferred_element_type=F32)
                for n in range(3)]
        ubuf_ref[CONV_HALO + rows.start:CONV_HALO + rows.stop, cols] = part[0] * jax.nn.sigmoid(part[1])
        gate_ref[rows, cols] = part[2]
        return gate_ref[rows.start:rows.start + SUBLANES, cols.start:cols.start + LANES]

    def conv_piece(r0, cols, token=None):
        acc = jnp.broadcast_to(cb_ref[:, cols], (CONV_ROWS, CONV_STRIP))
        for r in range(SUBLANES):
            nrows = CONV_ROWS + (SUBLANES if r else 0)
            part = None
            for a in range((base + CONV_K - 1) // SUBLANES + 1):
                k = SUBLANES * a + r - base
                if 0 <= k < CONV_K:
                    row = r0 + SUBLANES * a
                    term = cw_ref[k:k + 1, cols] * ubuf_ref[row:row + nrows, cols]
                    part = term if part is None else part + term
            acc = acc + part[r:r + CONV_ROWS, :]
        c_ref[r0:r0 + CONV_ROWS, cols] = acc
        if token is not None:
            tile = (slice(r0, r0 + SUBLANES), slice(cols.start, cols.start + LANES))
            c_ref[tile] = jnp.where(i < 0, token, c_ref[tile])

    def post_piece(r0):
        rows = slice(r0, r0 + CONV_CHUNK)
        acc = c_ref[rows, :]
        mu = jnp.mean(acc, axis=-1, keepdims=True)
        cen = acc - mu
        var = jnp.mean(cen * cen, axis=-1, keepdims=True)
        y = cen * lax.rsqrt(var + LN_EPS) * lng + lnb
        v_ref[rows, :] = (_silu(y) * _silu(gate_ref[rows, :])).astype(BF16)

    def out_piece(s, cols):
        rows = slice(CONV_SUB * s, CONV_SUB * (s + 1))
        o_ref[rows, cols] = h_ref[rows, cols] + jnp.dot(v_ref[rows, :], wout_ref[:, cols],
                                                        preferred_element_type=F32)
        return o_ref[rows.start:rows.start + SUBLANES, cols.start:cols.start + LANES]

    norm_piece(0)
    for cols in mstrips:
        proj_piece(0, cols)
    for s in range(nsub):
        side = []
        if s + 1 < nsub:
            norm_piece(s + 1)
            side += [functools.partial(proj_piece, s + 1, cols) for cols in mstrips]
        if s >= 1:
            side += [functools.partial(out_piece, s - 1, cols) for cols in mstrips]
        main = [functools.partial(conv_piece, r0, cols)
                for r0 in range(CONV_SUB * s, CONV_SUB * (s + 1), CONV_ROWS) for cols in strips]
        every = max(1, (len(main) - 1) // max(1, len(side)))
        token = None
        for n, piece in enumerate(main):
            piece(token=token)
            token = side.pop(0)() if (n % every == every - 1 and side) else None
        for piece in side:
            piece()
        for r0 in range(CONV_SUB * s, CONV_SUB * (s + 1), CONV_CHUNK):
            post_piece(r0)
    for cols in mstrips:
        out_piece(nsub - 1, cols)

    ubuf_ref[0:CONV_HALO, :] = ubuf_ref[tm:tm + CONV_HALO, :]


def _row_spec(tm, d, nt, seq_major):
    if seq_major:
        return pl.BlockSpec((tm, d), lambda b, i: (i, b))
    return pl.BlockSpec((tm, d), lambda b, i: (b * nt + i, 0))


def _conv_layer(h2d, bsz, seq, seq_major_in, seq_major_out, g, w_in, conv_w, conv_b, ln_g, ln_b, w_out, tm):
    d = g.shape[-1]
    width = conv_w.shape[-1]
    nt = seq // tm
    out_shape = (seq, bsz * d) if seq_major_out else (bsz * seq, d)
    kern = functools.partial(_conv_kernel, tm=tm, width=width)
    return pl.pallas_call(
        kern,
        grid=(bsz, nt),
        in_specs=[
            _row_spec(tm, d, nt, seq_major_in),
            _const_spec((1, d)),
            _const_spec((d, 3 * width)),
            _const_spec((CONV_K, width)),
            _const_spec((1, width)),
            _const_spec((1, width)),
            _const_spec((1, width)),
            _const_spec((width, d)),
        ],
        out_specs=_row_spec(tm, d, nt, seq_major_out),
        out_shape=jax.ShapeDtypeStruct(out_shape, F32),
        scratch_shapes=[pltpu.VMEM((tm + CONV_HALO, width), F32), pltpu.VMEM((tm, width), F32),
                        pltpu.VMEM((tm, width), BF16), pltpu.VMEM((tm, d), BF16), pltpu.VMEM((tm, width), F32)],
        compiler_params=_params(("arbitrary", "arbitrary")),
        name="conv_layer",
    )(h2d, g.reshape(1, d), w_in.astype(BF16), conv_w, conv_b.reshape(1, width),
      ln_g.reshape(1, width), ln_b.reshape(1, width), w_out.astype(BF16))


def _split3(x):
    p1 = x.astype(BF16)
    r1 = x - p1.astype(F32)
    p2 = r1.astype(BF16)
    r2 = r1 - p2.astype(F32)
    return p1, p2, r2.astype(BF16)


def _fox_proj_kernel(h_ref, g_ref, w_ref, wvt_ref, wf_ref, fb_ref, qg_ref, kg_ref,
                     seg_ref, tri_ref, place_ref, q_ref, k_ref, kc_ref, vt_ref, gate_ref, carry_ref,
                     *, tm, width):
    i = pl.program_id(1)
    heads = width // HEAD_DIM

    @pl.when(i == 0)
    def _():
        carry_ref[...] = jnp.zeros(carry_ref.shape, F32)

    hn = _rms(h_ref[...], g_ref[...]).astype(BF16)
    seg = seg_ref[...]

    def head_norm(x, gain):
        x2 = (x * x).astype(BF16)
        ss = jnp.concatenate([jnp.dot(x2[:, MXU_DIM * s:MXU_DIM * (s + 1)], seg, preferred_element_type=F32)
                              for s in range(width // MXU_DIM)], axis=-1)
        return x * lax.rsqrt(ss * (1.0 / HEAD_DIM) + RMS_EPS) * gain

    def w_in(n):
        return w_ref[:, n * width:(n + 1) * width]

    zq = jnp.dot(hn, w_in(0), preferred_element_type=F32)
    q_ref[...] = (head_norm(zq, qg_ref[...]) * (LOG2E / math.sqrt(HEAD_DIM))).astype(BF16)
    zk = jnp.dot(hn, w_in(1), preferred_element_type=F32)
    k_ref[...] = head_norm(zk, kg_ref[...]).astype(BF16)
    vt_ref[...] = lax.dot_general(wvt_ref[...], hn, NT_DIMS, preferred_element_type=F32).astype(BF16)
    gate_ref[...] = jnp.dot(hn, w_in(3), preferred_element_type=F32)

    zf = jnp.dot(hn, wf_ref[...], preferred_element_type=F32) + fb_ref[...]
    logf = jnp.minimum(zf, 0.0) - jnp.log1p(jnp.exp(-jnp.abs(zf)))
    tri = tri_ref[...]
    p1, p2, p3 = _split3(logf)
    local = (jnp.dot(tri, p1, preferred_element_type=F32)
             + jnp.dot(tri, p2, preferred_element_type=F32)
             + jnp.dot(tri, p3, preferred_element_type=F32))
    cum = local + carry_ref[0:1, :]
    carry_ref[...] = jnp.broadcast_to(cum[tm - 1:tm, :], carry_ref.shape)
    d1, d2, d3 = _split3(cum * (-LOG2E))
    lane = lax.broadcasted_iota(jnp.int32, (tm, LANES), 1)
    parts = jnp.where(lane < heads, d1, jnp.where(lane < 2 * heads, d2, d3))
    kc_ref[...] = jnp.dot(parts, place_ref[...], preferred_element_type=F32).astype(BF16)


def _fox_proj(h2d, bsz, seq, g, w_in, f_bias, q_g, k_g, tm):
    d = g.shape[-1]
    width = (w_in.shape[-1] // (4 * HEAD_DIM + 1)) * HEAD_DIM
    heads = width // HEAD_DIM
    assert DECAY_PARTS * heads <= LANES
    nt = seq // tm
    wb = w_in.astype(BF16)
    wvt = wb[:, 2 * width:3 * width].T
    wf = jnp.pad(jnp.tile(wb[:, 4 * width:], (1, DECAY_PARTS)), ((0, 0), (0, LANES - DECAY_PARTS * heads)))
    fb = jnp.pad(jnp.tile(f_bias, DECAY_PARTS), (0, LANES - DECAY_PARTS * heads)).reshape(1, LANES)
    seg_id = jnp.arange(MXU_DIM) // HEAD_DIM
    seg = (seg_id[:, None] == seg_id[None, :]).astype(BF16)
    tri = (jnp.arange(tm)[:, None] >= jnp.arange(tm)[None, :]).astype(BF16)
    src = jnp.arange(LANES)
    part, head = src // heads, src % heads
    dst = (head // HEADS_PER_SLAB) * LANES + DECAY_PARTS * (head % HEADS_PER_SLAB) + part
    place = ((dst[:, None] == jnp.arange(width)[None, :]) & (src[:, None] < DECAY_PARTS * heads)).astype(BF16)
    row = lambda n: pl.BlockSpec((tm, n), lambda b, i: (b * nt + i, 0))
    kern = functools.partial(_fox_proj_kernel, tm=tm, width=width)
    m = bsz * seq
    return pl.pallas_call(
        kern,
        grid=(bsz, nt),
        in_specs=[row(d), _const_spec((1, d)), _const_spec(wb.shape), _const_spec((width, d)),
                  _const_spec((d, LANES)), _const_spec((1, LANES)), _const_spec((1, width)), _const_spec((1, width)),
                  _const_spec((MXU_DIM, MXU_DIM)), _const_spec((tm, tm)), _const_spec((LANES, width))],
        out_specs=[row(width), row(width), row(width),
                   pl.BlockSpec((None, None, width, tm), lambda b, i: (b, i, 0, 0)), row(width)],
        out_shape=[jax.ShapeDtypeStruct((m, width), BF16)] * 3
        + [jax.ShapeDtypeStruct((bsz, nt, width, tm), BF16), jax.ShapeDtypeStruct((m, width), F32)],
        scratch_shapes=[pltpu.VMEM((SUBLANES, LANES), F32)],
        compiler_params=_params(("arbitrary", "arbitrary")),
        name="fox_proj",
    )(h2d, g.reshape(1, d), wb, wvt, wf, fb,
      jnp.tile(q_g, heads).reshape(1, width), jnp.tile(k_g, heads).reshape(1, width), seg, tri, place)


def _attn_kernel(q_ref, k_ref, kc_ref, vt_ref, gate_a_ref, gate_b_ref, oa_ref, ob_ref,
                 qt_ref, sta_ref, stb_ref, acc_ref, mx_ref, cm_ref, *, tq, npairs, nq):
    i = pl.program_id(2)
    half = npairs // 2
    nch = HEADS_PER_SLAB * half
    st_refs = (sta_ref, stb_ref)
    lane = lax.broadcasted_iota(jnp.int32, (tq, LANES), 1)
    on_or_below_diagonal = (lax.broadcasted_iota(jnp.int32, (tq, tq), 0)
                            <= lax.broadcasted_iota(jnp.int32, (tq, tq), 1))

    def band(lo, hi):
        return jnp.where(lane < hi, 1.0, 0.0) * jnp.where(lane < lo, 0.0, 1.0)

    ones = jnp.ones((ONES_ROWS, tq), BF16)

    def score_piece(hf, n, j, diagonal):
        c = nch * hf + n
        slab = slice(LANES * (c // HEADS_PER_SLAB), LANES * (c // HEADS_PER_SLAB + 1))
        rows = pl.ds(pl.multiple_of(j * tq, tq), tq)
        k_aug = jnp.concatenate([k_ref[rows, slab], kc_ref[rows, slab]], axis=-1)
        st = jnp.dot(k_aug, qt_ref[c], preferred_element_type=F32)
        if diagonal:
            st = jnp.where(on_or_below_diagonal, st, -jnp.inf)
        st_refs[hf][n, j] = st
        mx_ref[c] = jnp.maximum(mx_ref[c], jnp.max(st.reshape(tq // SUBLANES, SUBLANES, tq), axis=0))

    def acc_piece(hf, n, j):
        c = nch * hf + n
        v0 = HEAD_DIM * c
        v_aug = jnp.concatenate([vt_ref[j, v0:v0 + HEAD_DIM, :], ones], axis=0)
        pt = jnp.exp2(st_refs[hf][n, j] - cm_ref[c]).astype(BF16)
        acc_ref[c] += jnp.dot(v_aug, pt, preferred_element_type=F32)

    def scores(hf, j0, count, diagonal):
        for u in range(count):
            for n in range(nch):
                score_piece(hf, n, j0 + u, diagonal)

    def accumulate(hf, j0, count):
        for u in range(count):
            for n in range(nch):
                acc_piece(hf, n, j0 + u)

    def both(score_half, j0, count, diagonal):
        for u in range(count):
            for n in range(nch):
                score_piece(score_half, n, j0 + u, diagonal)
                acc_piece(1 - score_half, n, j0 + u)

    def grouped(fn, nblocks):
        ngroups = nblocks // ATTN_UNROLL

        def body(g, carry):
            fn(g * ATTN_UNROLL, ATTN_UNROLL)
            return carry
        lax.fori_loop(0, ngroups, body, 0)
        for rem in range(1, ATTN_UNROLL):
            @pl.when(nblocks - ngroups * ATTN_UNROLL == rem)
            def _(rem=rem):
                fn(ngroups * ATTN_UNROLL, rem)

    def reset(hf, ref, value):
        ref[nch * hf:nch * (hf + 1)] = jnp.full((nch,) + ref.shape[1:], value, F32)

    def column_max(hf):
        for c in range(nch * hf, nch * (hf + 1)):
            cm_ref[c] = jnp.max(mx_ref[c], axis=0, keepdims=True)

    def finish(hf, gate_ref, o_ref):
        for p in range(half):
            halves = []
            for e in range(HEADS_PER_SLAB):
                acc = acc_ref[nch * hf + HEADS_PER_SLAB * p + e]
                halves.append(acc[:HEAD_DIM, :] / acc[HEAD_DIM:HEAD_DIM + 1, :])
            o = jnp.concatenate(halves, axis=0).T
            slab = slice(LANES * p, LANES * (p + 1))
            o_ref[:, slab] = (o * _silu(gate_ref[:, slab])).astype(BF16)

    reset(1, acc_ref, 0.0)

    @pl.when(i < nq)
    def _():
        for p in range(npairs):
            q = q_ref[:, LANES * p:LANES * (p + 1)].astype(F32)
            for e in range(HEADS_PER_SLAB):
                q_aug = jnp.concatenate(
                    [q * band(HEAD_DIM * e, HEAD_DIM * (e + 1)), band(DECAY_PARTS * e, DECAY_PARTS * (e + 1))],
                    axis=-1)
                qt_ref[HEADS_PER_SLAB * p + e] = q_aug.T.astype(BF16)
        reset(0, mx_ref, -jnp.inf)

        grouped(lambda j0, count: both(0, j0, count, False), i)
        scores(0, i, 1, True)

    @pl.when(i == nq)
    def _():
        grouped(functools.partial(accumulate, 1), nq)

    @pl.when(i >= 1)
    def _():
        finish(1, gate_b_ref, ob_ref)

    @pl.when(i < nq)
    def _():
        column_max(0)
        reset(0, acc_ref, 0.0)
        reset(1, mx_ref, -jnp.inf)

        grouped(lambda j0, count: both(1, j0, count, False), i)
        both(1, i, 1, True)
        finish(0, gate_a_ref, oa_ref)
        column_max(1)


def _fox_attn(q, k, kc, vt, gate, bsz, seq, tq):
    m, width = q.shape
    nq = seq // tq
    npairs = min(ATTN_PAIRS, width // LANES)
    assert npairs % 2 == 0
    nchains = HEADS_PER_SLAB * npairs
    cols = npairs * LANES
    hcols = cols // 2
    cur = lambda i: jnp.minimum(i, nq - 1)
    prev = lambda i: jnp.maximum(i - 1, 0)
    kspec = pl.BlockSpec((seq, cols), lambda b, p, i: (b, p))
    oa, ob = pl.pallas_call(
        functools.partial(_attn_kernel, tq=tq, npairs=npairs, nq=nq),
        grid=(bsz, width // cols, nq + 1),
        in_specs=[pl.BlockSpec((tq, cols), lambda b, p, i: (b * nq + cur(i), p)), kspec, kspec,
                  pl.BlockSpec((None, nq, cols, tq), lambda b, p, i: (b, 0, p, 0)),
                  pl.BlockSpec((tq, hcols), lambda b, p, i: (b * nq + cur(i), 2 * p)),
                  pl.BlockSpec((tq, hcols), lambda b, p, i: (b * nq + prev(i), 2 * p + 1))],
        out_specs=[pl.BlockSpec((tq, hcols), lambda b, p, i: (b * nq + cur(i), p)),
                   pl.BlockSpec((tq, hcols), lambda b, p, i: (b * nq + prev(i), p))],
        out_shape=[jax.ShapeDtypeStruct((m, width // 2), BF16)] * 2,
        scratch_shapes=[pltpu.VMEM((nchains, 2 * LANES, tq), BF16),
                        pltpu.VMEM((nchains // 2, nq, tq, tq), F32),
                        pltpu.VMEM((nchains // 2, nq, tq, tq), F32),
                        pltpu.VMEM((nchains, HEAD_DIM + ONES_ROWS, tq), F32),
                        pltpu.VMEM((nchains, SUBLANES, tq), F32),
                        pltpu.VMEM((nchains, 1, tq), F32)],
        compiler_params=_params(("arbitrary", "arbitrary", "arbitrary")),
        name="fox_attn",
    )(q, k, kc, vt, gate, gate)
    return oa, ob, width // cols


def _out_proj_kernel(xa_ref, xb_ref, wa_ref, wb_ref, h_ref, o_ref):
    o_ref[...] = (h_ref[...] + jnp.dot(xa_ref[...], wa_ref[...], preferred_element_type=F32)
                  + jnp.dot(xb_ref[...], wb_ref[...], preferred_element_type=F32))


def _out_proj(xa, xb, ngroups, w_out, h2d, bsz, seq, seq_major_out, tm):
    width, d = w_out.shape
    nt = seq // tm
    w = w_out.astype(BF16).reshape(ngroups, 2, width // (2 * ngroups), d)
    wa, wb = (w[:, n].reshape(width // 2, d) for n in range(2))
    out_shape = (seq, bsz * d) if seq_major_out else (bsz * seq, d)
    xspec = _row_spec(tm, width // 2, nt, False)
    return pl.pallas_call(
        _out_proj_kernel,
        grid=(bsz, nt),
        in_specs=[xspec, xspec, _const_spec((width // 2, d)), _const_spec((width // 2, d)),
                  _row_spec(tm, d, nt, False)],
        out_specs=_row_spec(tm, d, nt, seq_major_out),
        out_shape=jax.ShapeDtypeStruct(out_shape, F32),
        compiler_params=_params(("arbitrary", "arbitrary")),
        name="fox_out",
    )(xa, xb, wa, wb, h2d)


SSM_CHUNK_GROUPS = LANES // SSM_STATE
SSM_CHUNK_COLS = 2 * LANES
SSM_CHUNK_CH = SSM_CHUNK_GROUPS * SSM_GROUP
SSM_SCAN_CHUNKS = 2
SSM_SUB = 2


def _ssm_kernel(h_ref, g_ref, win_ref, bmat_ref, ar_ref, ai_ref, cmat_ref, d_ref, wglu_ref, bglu_ref,
                wout_ref, o_ref, bu_ref, state_ref, coef_ref, hn_ref, u_ref, ub_ref, gate_ref, gl_ref, v_ref,
                hin_ref, hout_ref, *, ts, nb, width):
    i = pl.program_id(0)
    rows = ts * nb
    rs, tsub = rows // SSM_SUB, ts // SSM_SUB
    nchunk = width // SSM_CHUNK_CH
    per_slab = LANES // SSM_CHUNK_CH
    ngroup = nchunk // SSM_SCAN_CHUNKS
    strips = [slice(c0, c0 + MXU_DIM) for c0 in range(0, width, MXU_DIM)]

    @pl.when(i == 0)
    def _():
        state_ref[...] = jnp.zeros(state_ref.shape, F32)
        coef_ref[0] = jnp.broadcast_to(ar_ref[...], coef_ref.shape[1:])
        coef_ref[1] = jnp.broadcast_to(ai_ref[...], coef_ref.shape[1:])

    def sub_rows(s):
        return slice(rs * s, rs * (s + 1))

    def batch_rows(s, b):
        return pl.ds(rs * s + b, tsub, stride=nb)

    def norm_piece(s):
        t = slice(tsub * s, tsub * (s + 1))
        for b in range(nb):
            for j in range(width // LANES):
                hin_ref[j, batch_rows(s, b), :] = h_ref[t, width * b + LANES * j:width * b + LANES * (j + 1)]
        h = jnp.concatenate([hin_ref[j, sub_rows(s), :] for j in range(width // LANES)], axis=-1)
        hn_ref[sub_rows(s), :] = _rms(h, g_ref[...]).astype(BF16)

    def in_piece(s, cols):
        r = sub_rows(s)
        hn = hn_ref[r, :]
        u = jnp.dot(hn, win_ref[:, cols], preferred_element_type=F32)
        u_ref[r, cols] = u
        ub_ref[r, cols] = u.astype(BF16)
        gate_ref[r, cols] = jnp.dot(hn, win_ref[:, width + cols.start:width + cols.stop],
                                    preferred_element_type=F32)
        return gate_ref[r.start:r.start + nb, cols.start:cols.start + LANES]

    def bproj_piece(s, c):
        r = sub_rows(s)
        slab = c // per_slab
        ccols = slice(SSM_CHUNK_COLS * c, SSM_CHUNK_COLS * (c + 1))
        bu_ref[r, ccols] = jnp.dot(ub_ref[r, LANES * slab:LANES * (slab + 1)], bmat_ref[c],
                                   preferred_element_type=F32)
        return bu_ref[r.start:r.start + nb, ccols.start:ccols.start + LANES]

    def scan_piece(s, t, g, carry, token):
        r = slice(rs * s + nb * t, rs * s + nb * (t + 1))
        new = []
        for n, c in enumerate(range(SSM_SCAN_CHUNKS * g, SSM_SCAN_CHUNKS * (g + 1))):
            xr, xi = carry[2 * n], carry[2 * n + 1]
            lanes = slice(LANES * c, LANES * (c + 1))
            ar, ai = coef_ref[0, :, lanes], coef_ref[1, :, lanes]
            re_cols = slice(SSM_CHUNK_COLS * c, SSM_CHUNK_COLS * c + LANES)
            im_cols = slice(SSM_CHUNK_COLS * c + LANES, SSM_CHUNK_COLS * (c + 1))
            nr = ar * xr - ai * xi + bu_ref[r, re_cols]
            ni = ar * xi + ai * xr + bu_ref[r, im_cols]
            if n == 0 and token is not None:
                nr = jnp.where(i < 0, token, nr)
            bu_ref[r, re_cols] = nr
            bu_ref[r, im_cols] = ni
            new += [nr, ni]
        return new

    def cproj_piece(s, slab):
        r = sub_rows(s)
        cols = slice(LANES * slab, LANES * (slab + 1))
        acc = None
        for c in range(per_slab * slab, per_slab * (slab + 1)):
            xc = bu_ref[r, SSM_CHUNK_COLS * c:SSM_CHUNK_COLS * (c + 1)].astype(BF16)
            part = jnp.dot(xc, cmat_ref[c], preferred_element_type=F32)
            acc = part if acc is None else acc + part
        gl = jax.nn.gelu(acc + d_ref[:, cols] * u_ref[r, cols])
        u_ref[r, cols] = gl
        gl_ref[r, cols] = gl.astype(BF16)
        return u_ref[r.start:r.start + nb, cols]

    def glu_piece(s, cols):
        r = sub_rows(s)
        t2 = jnp.dot(gl_ref[r, :], wglu_ref[:, cols], preferred_element_type=F32) + bglu_ref[:, cols]
        y = u_ref[r, cols] * jax.nn.sigmoid(t2)
        v_ref[r, cols] = (y * _silu(gate_ref[r, cols])).astype(BF16)
        return None

    def out_piece(s, cols):
        r = sub_rows(s)
        t = slice(tsub * s, tsub * (s + 1))
        planes = range(cols.start // LANES, cols.stop // LANES)
        h = jnp.concatenate([hin_ref[j, r, :] for j in planes], axis=-1)
        out = h + jnp.dot(v_ref[r, :], wout_ref[:, cols], preferred_element_type=F32)
        for n, j in enumerate(planes):
            hout_ref[j, r, :] = out[:, LANES * n:LANES * (n + 1)]
        for b in range(nb):
            for j in planes:
                o_ref[t, width * b + LANES * j:width * b + LANES * (j + 1)] = hout_ref[j, batch_rows(s, b), :]
        return o_ref[t.start:t.start + nb, cols.start:cols.start + LANES]

    def before(s):
        return ([functools.partial(in_piece, s, cols) for cols in strips]
                + [functools.partial(bproj_piece, s, c) for c in range(nchunk)])

    def after(s):
        return ([functools.partial(cproj_piece, s, slab) for slab in range(width // LANES)]
                + [functools.partial(glu_piece, s, cols) for cols in strips]
                + [functools.partial(out_piece, s, cols) for cols in strips])

    norm_piece(0)
    for piece in before(0):
        piece()
    carry = [[state_ref[:, LANES * k:LANES * (k + 1)]
              for k in range(2 * SSM_SCAN_CHUNKS * g, 2 * SSM_SCAN_CHUNKS * (g + 1))] for g in range(ngroup)]
    for s in range(SSM_SUB):
        side = []
        if s + 1 < SSM_SUB:
            norm_piece(s + 1)
            side += before(s + 1)
        if s >= 1:
            side += after(s - 1)
        main = [(t, g) for t in range(tsub) for g in range(ngroup)]
        every = max(1, (len(main) - 1) // max(1, len(side)))
        token = None
        for n, (t, g) in enumerate(main):
            carry[g] = scan_piece(s, t, g, carry[g], token)
            token = side.pop(0)() if (n % every == every - 1 and side) else None
        for piece in side:
            piece()
    for piece in after(SSM_SUB - 1):
        piece()
    for g in range(ngroup):
        for n, k in enumerate(range(2 * SSM_SCAN_CHUNKS * g, 2 * SSM_SCAN_CHUNKS * (g + 1))):
            state_ref[:, LANES * k:LANES * (k + 1)] = carry[g][n]


def _ssm_discretize(log_dt, a_re, a_im, b_re, b_im, c_re, c_im):
    groups, state = a_re.shape
    nchunk = groups // SSM_CHUNK_GROUPS
    per_slab = LANES // SSM_CHUNK_CH
    dt = jnp.exp(log_dt)[:, None]
    mag = jnp.exp(a_re * dt)
    ang = a_im * dt
    abar_re = mag * jnp.cos(ang)
    abar_im = mag * jnp.sin(ang)
    den = a_re * a_re + a_im * a_im
    nr = abar_re - 1.0
    ni = abar_im
    zr = ((nr * a_re + ni * a_im) / den)[..., None]
    zi = ((ni * a_re - nr * a_im) / den)[..., None]
    bb_re = zr * b_re - zi * b_im
    bb_im = zr * b_im + zi * b_re
    eye = jnp.eye(SSM_CHUNK_GROUPS, dtype=F32)
    slot = jax.nn.one_hot(jnp.arange(nchunk) % per_slab, per_slab, dtype=F32)

    def b_block(bb):
        x = bb.transpose(0, 2, 1).reshape(nchunk, SSM_CHUNK_GROUPS, SSM_GROUP, state)
        return jnp.einsum("cgip,gh->cgihp", x, eye).reshape(nchunk, SSM_CHUNK_CH, LANES)

    bblk = jnp.concatenate([b_block(bb_re), b_block(bb_im)], axis=-1)
    bmat = jnp.einsum("cq,crn->cqrn", slot, bblk).reshape(nchunk, LANES, SSM_CHUNK_COLS)

    def c_block(cc):
        x = cc.transpose(0, 2, 1).reshape(nchunk, SSM_CHUNK_GROUPS, state, SSM_GROUP)
        return jnp.einsum("cgpo,gh->cgpho", x, eye).reshape(nchunk, LANES, SSM_CHUNK_CH)

    cblk = jnp.concatenate([c_block(c_re), -c_block(c_im)], axis=1)
    cmat = jnp.einsum("cq,ckr->ckqr", slot, cblk).reshape(nchunk, SSM_CHUNK_COLS, LANES)
    return (bmat.astype(BF16), abar_re.reshape(1, groups * state), abar_im.reshape(1, groups * state),
            cmat.astype(BF16))


def _ssm_layer(h2d, nb, g, w_in, log_dt, a_re, a_im, b_re, b_im, c_re, c_im, d_skip, w_glu, b_glu, w_out, ts):
    seq, d = h2d.shape[0], h2d.shape[1] // nb
    width = d_skip.shape[-1]
    groups, state = a_re.shape
    assert nb == SUBLANES and state == SSM_STATE and width == groups * SSM_GROUP
    nchunk = groups // SSM_CHUNK_GROUPS
    assert nchunk % SSM_SCAN_CHUNKS == 0
    bmat, ar, ai, cmat = _ssm_discretize(log_dt, a_re, a_im, b_re, b_im, c_re, c_im)
    ncols = groups * state
    blk = pl.BlockSpec((ts, nb * d), lambda i: (i, 0))
    kern = functools.partial(_ssm_kernel, ts=ts, nb=nb, width=width)
    return pl.pallas_call(
        kern,
        grid=(seq // ts,),
        in_specs=[blk, _const_spec((1, d)), _const_spec((d, 2 * width)),
                  _const_spec((nchunk, LANES, SSM_CHUNK_COLS)), _const_spec((1, ncols)), _const_spec((1, ncols)),
                  _const_spec((nchunk, SSM_CHUNK_COLS, LANES)), _const_spec((1, width)),
                  _const_spec((width, width)), _const_spec((1, width)), _const_spec((width, d))],
        out_specs=blk,
        out_shape=jax.ShapeDtypeStruct((seq, nb * d), F32),
        scratch_shapes=[pltpu.VMEM((ts * nb, 2 * ncols), F32), pltpu.VMEM((nb, 2 * ncols), F32),
                        pltpu.VMEM((2, nb, ncols), F32), pltpu.VMEM((ts * nb, d), BF16),
                        pltpu.VMEM((ts * nb, width), F32), pltpu.VMEM((ts * nb, width), BF16),
                        pltpu.VMEM((ts * nb, width), F32), pltpu.VMEM((ts * nb, width), BF16),
                        pltpu.VMEM((ts * nb, width), BF16),
                        pltpu.VMEM((d // LANES, ts * nb, LANES), F32), pltpu.VMEM((d // LANES, ts * nb, LANES), F32)],
        compiler_params=_params(("arbitrary",)),
        name="ssm_layer",
    )(h2d, g.reshape(1, d), w_in.astype(BF16), bmat, ar, ai, cmat, d_skip.reshape(1, width),
      w_glu.astype(BF16), b_glu.reshape(1, width), w_out.astype(BF16))


def kernel(x, norm_g, a_w_in, a_conv_w, a_conv_b, a_ln_g, a_ln_b, a_w_out, b_w_in, b_f_bias, b_q_norm,
           b_k_norm, b_w_out, c_w_in, c_log_dt, c_a_re, c_a_im, c_b_re, c_b_im, c_c_re, c_c_im, c_d,
           c_w_glu, c_b_glu, c_w_out):
    bsz, seq, d = x.shape
    depth = norm_g.shape[0]
    tm = min(ROW_TILE, seq)
    tq = min(ATTN_TILE, seq)
    ts = min(SSM_STEPS, seq)
    h = x.reshape(bsz * seq, d)
    seq_major = False
    for layer in range(depth):
        kind, j = layer % 3, layer // 3
        next_major = (layer + 1 < depth) and ((layer + 1) % 3 == 2)
        if kind == 0:
            h = _conv_layer(h, bsz, seq, seq_major, next_major, norm_g[layer], a_w_in[j], a_conv_w[j],
                            a_conv_b[j], a_ln_g[j], a_ln_b[j], a_w_out[j], tm)
        elif kind == 1:
            assert not seq_major
            q, k, kc, vt, gate = _fox_proj(h, bsz, seq, norm_g[layer], b_w_in[j], b_f_bias[j],
                                           b_q_norm[j], b_k_norm[j], tq)
            oa, ob, ngroups = _fox_attn(q, k, kc, vt, gate, bsz, seq, tq)
            h = _out_proj(oa, ob, ngroups, b_w_out[j], h, bsz, seq, next_major, tm)
        else:
            assert seq_major
            h = _ssm_layer(h, bsz, norm_g[layer], c_w_in[j], c_log_dt[j], c_a_re[j],
                           c_a_im[j], c_b_re[j], c_b_im[j], c_c_re[j], c_c_im[j], c_d[j], c_w_glu[j],
                           c_b_glu[j], c_w_out[j], ts)
            next_major = True
        seq_major = next_major
    if seq_major:
        h = h.reshape(seq, bsz, d).transpose(1, 0, 2)
    return h.reshape(bsz, seq, d)
```
